```python
import math
import jax, jax.numpy as jnp
from jax import lax
import numpy as np

D_MODEL = 1024
BATCH = 4
SEQ = 4096
DEPTH = 1

CHUNK = 64
RNN_WIDTH = 1024
RNN_BLOCKS = 16
RNN_BLOCK_DIM = RNN_WIDTH // RNN_BLOCKS
CONV_WIDTH = 4
LRU_C = 8.0
FOX_HEADS = 16
FOX_HEAD_DIM = 64
FOX_WIDTH = FOX_HEADS * FOX_HEAD_DIM
QUERY_BLOCK = 128
N_BRANCHES = 2
IN_COLS = 2 * RNN_WIDTH + 3 * FOX_WIDTH + FOX_HEADS + N_BRANCHES * D_MODEL
N_EXPERTS = 32
TOP_K = 4
D_FF = 1024
SWIGLU_LIMIT = 7.0
SWIGLU_ALPHA = 1.702
RMS_EPS = 1e-6

kernel_name = "hybrid_rglru_fox_moe_block"


def rmsnorm(x, g):
    xf = x.astype(jnp.float32)
    y = xf * lax.rsqrt(jnp.mean(xf * xf, axis=-1, keepdims=True) + RMS_EPS)
    return (y * g.astype(jnp.float32)).astype(x.dtype)


def causal_depthwise_conv(x, w, b):
    c = x.shape[-1]
    y = lax.conv_general_dilated(
        x, w[:, None, :].astype(x.dtype), window_strides=(1,),
        padding=[(CONV_WIDTH - 1, 0)],
        dimension_numbers=("NWC", "WIO", "NWC"), feature_group_count=c)
    return y + b.astype(x.dtype)


def _lin_combine(left, right):
    a_l, b_l = left
    a_r, b_r = right
    return a_l * a_r, a_r * b_l + b_r


def rg_lru(xc, w_rg, b_rg, w_ig, b_ig, lru_lambda):
    bsz, s, r = xc.shape
    xb = xc.reshape(bsz, s, RNN_BLOCKS, RNN_BLOCK_DIM)
    gate_r = jnp.einsum('bshi,hij->bshj', xb, w_rg).reshape(bsz, s, r) + b_rg
    gate_i = jnp.einsum('bshi,hij->bshj', xb, w_ig).reshape(bsz, s, r) + b_ig
    rt = jax.nn.sigmoid(gate_r.astype(jnp.float32))
    it = jax.nn.sigmoid(gate_i.astype(jnp.float32))
    log_a = -LRU_C * rt * jax.nn.softplus(-lru_lambda.astype(jnp.float32))
    a = jnp.exp(log_a)
    mult = jnp.sqrt(-jnp.expm1(2.0 * log_a))
    bterm = mult * (it * xc.astype(jnp.float32))
    nc = s // CHUNK
    a_c = a.reshape(bsz, nc, CHUNK, r).transpose(1, 0, 2, 3)
    b_c = bterm.reshape(bsz, nc, CHUNK, r).transpose(1, 0, 2, 3)

    def step(h0, ab):
        ac, bc = ab
        acum, bcum = lax.associative_scan(_lin_combine, (ac, bc), axis=1)
        h = acum * h0[:, None, :] + bcum
        return h[:, -1, :], h

    h0 = jnp.zeros((bsz, r), jnp.float32)
    _, hs = lax.scan(step, h0, (a_c, b_c))
    return hs.transpose(1, 0, 2, 3).reshape(bsz, s, r)


def forgetting_attention(q, k, v, cum_logf):
    s = q.shape[2]
    scale = 1.0 / math.sqrt(FOX_HEAD_DIM)
    outs = []
    for i in range(s // QUERY_BLOCK):
        q0 = i * QUERY_BLOCK
        q1 = q0 + QUERY_BLOCK
        logits = jnp.einsum('bhqd,bhkd->bhqk', q[:, :, q0:q1].astype(jnp.float32),
                            k[:, :, :q1].astype(jnp.float32)) * scale
        logits = logits + (cum_logf[:, :, q0:q1, None] - cum_logf[:, :, None, :q1])
        qpos = jnp.arange(q0, q1)[:, None]
        kpos = jnp.arange(q1)[None, :]
        logits = jnp.where(kpos <= qpos, logits, -jnp.inf)
        p = jax.nn.softmax(logits, axis=-1).astype(v.dtype)
        outs.append(jnp.einsum('bhqk,bhkd->bhqd', p, v[:, :, :q1]))
    return jnp.concatenate(outs, axis=2)


def moe_ffn(h, w_router, b_router, w_gate_up, b_gate_up, w_down, b_down):
    bsz, s, d = h.shape
    t = h.reshape(bsz * s, d)
    logits = (t @ w_router).astype(jnp.float32) + b_router.astype(jnp.float32)
    top_v, top_i = lax.top_k(logits, TOP_K)
    top_w = jax.nn.softmax(top_v, axis=-1)
    combine = jnp.sum(jax.nn.one_hot(top_i, N_EXPERTS, dtype=jnp.float32) * top_w[..., None], axis=1)
    combine = combine.astype(h.dtype)
    out = jnp.zeros_like(t)
    for e in range(N_EXPERTS):
        gu = t @ w_gate_up[e] + b_gate_up[e]
        gate = jnp.minimum(gu[:, :D_FF], SWIGLU_LIMIT)
        up = jnp.clip(gu[:, D_FF:], -SWIGLU_LIMIT, SWIGLU_LIMIT)
        glu = gate * jax.nn.sigmoid(SWIGLU_ALPHA * gate)
        y = ((up + 1.0) * glu) @ w_down[e] + b_down[e]
        out = out + combine[:, e:e + 1] * y
    return out.reshape(bsz, s, d)


def setup_inputs(seed: int = 0) -> dict:
    key = jax.random.key(seed)
    ks = jax.random.split(key, 24)
    f32 = jnp.float32
    nrm = lambda k, shape, scale: jax.random.normal(k, shape, f32) * scale
    u = jax.random.uniform(ks[9], (RNN_WIDTH,), f32, 0.9, 0.999)
    sig_l = u ** (1.0 / LRU_C)
    lru_lambda = jnp.log(sig_l) - jnp.log1p(-sig_l)
    return {
        "x": nrm(ks[0], (BATCH, SEQ, D_MODEL), 1.0),
        "g_mix": 1.0 + nrm(ks[1], (D_MODEL,), 0.02),
        "w_in": nrm(ks[2], (D_MODEL, IN_COLS), D_MODEL ** -0.5),
        "conv_w": nrm(ks[3], (CONV_WIDTH, RNN_WIDTH), CONV_WIDTH ** -0.5),
        "conv_b": nrm(ks[4], (RNN_WIDTH,), 0.01),
        "w_rg": nrm(ks[5], (RNN_BLOCKS, RNN_BLOCK_DIM, RNN_BLOCK_DIM), RNN_BLOCK_DIM ** -0.5),
        "b_rg": nrm(ks[6], (RNN_WIDTH,), 0.01),
        "w_ig": nrm(ks[7], (RNN_BLOCKS, RNN_BLOCK_DIM, RNN_BLOCK_DIM), RNN_BLOCK_DIM ** -0.5),
        "b_ig": nrm(ks[8], (RNN_WIDTH,), 0.01),
        "lru_lambda": lru_lambda,
        "b_f": 2.0 + nrm(ks[10], (FOX_HEADS,), 0.1),
        "w_lru_out": nrm(ks[11], (RNN_WIDTH, D_MODEL), RNN_WIDTH ** -0.5),
        "w_fox_out": nrm(ks[12], (FOX_WIDTH, D_MODEL), FOX_WIDTH ** -0.5),
        "w_o": nrm(ks[13], (D_MODEL, D_MODEL), D_MODEL ** -0.5),
        "g_ffn": 1.0 + nrm(ks[14], (D_MODEL,), 0.02),
        "w_router": nrm(ks[15], (D_MODEL, N_EXPERTS), D_MODEL ** -0.5),
        "b_router": nrm(ks[16], (N_EXPERTS,), 0.01),
        "w_gate_up": nrm(ks[17], (N_EXPERTS, D_MODEL, 2 * D_FF), D_MODEL ** -0.5),
        "b_gate_up": nrm(ks[18], (N_EXPERTS, 2 * D_FF), 0.01),
        "w_down": nrm(ks[19], (N_EXPERTS, D_FF, D_MODEL), D_FF ** -0.5),
        "b_down": nrm(ks[20], (N_EXPERTS, D_MODEL), 0.01),
        "g_final": 1.0 + nrm(ks[21], (D_MODEL,), 0.02),
    }


def reference(x, g_mix, w_in, conv_w, conv_b, w_rg, b_rg, w_ig, b_ig, lru_lambda, b_f,
              w_lru_out, w_fox_out, w_o, g_ffn, w_router, b_router, w_gate_up, b_gate_up,
              w_down, b_down, g_final):
    bsz, s, _ = x.shape
    for _layer in range(DEPTH):
        h = rmsnorm(x, g_mix)
        proj = h @ w_in
        o = 0
        lru_x = proj[..., o:o + RNN_WIDTH]; o += RNN_WIDTH
        lru_gate = proj[..., o:o + RNN_WIDTH]; o += RNN_WIDTH
        q = proj[..., o:o + FOX_WIDTH]; o += FOX_WIDTH
        k = proj[..., o:o + FOX_WIDTH]; o += FOX_WIDTH
        v = proj[..., o:o + FOX_WIDTH]; o += FOX_WIDTH
        f_logit = proj[..., o:o + FOX_HEADS]; o += FOX_HEADS
        gate_lru = proj[..., o:o + D_MODEL]; o += D_MODEL
        gate_fox = proj[..., o:o + D_MODEL]

        xc = causal_depthwise_conv(lru_x, conv_w, conv_b)
        hr = rg_lru(xc, w_rg, b_rg, w_ig, b_ig, lru_lambda)
        y_lru = (hr * jax.nn.gelu(lru_gate.astype(jnp.float32), approximate=True)).astype(x.dtype)

        to_heads = lambda z: z.reshape(bsz, s, FOX_HEADS, FOX_HEAD_DIM).transpose(0, 2, 1, 3)
        logf = jax.nn.log_sigmoid(f_logit.astype(jnp.float32) + b_f.astype(jnp.float32))
        cum_logf = jnp.cumsum(logf, axis=1).transpose(0, 2, 1)
        attn = forgetting_attention(to_heads(q), to_heads(k), to_heads(v), cum_logf)
        y_fox = attn.transpose(0, 2, 1, 3).reshape(bsz, s, FOX_WIDTH)

        merged = (jax.nn.sigmoid(gate_lru) * (y_lru @ w_lru_out)
                  + jax.nn.sigmoid(gate_fox) * (y_fox @ w_fox_out))
        x = x + merged @ w_o

        x = x + moe_ffn(rmsnorm(x, g_ffn), w_router, b_router, w_gate_up, b_gate_up, w_down, b_down)
    return rmsnorm(x, g_final)
```

```python
import functools
import math

import jax
import jax.numpy as jnp
from jax import lax
from jax.experimental import pallas as pl
from jax.experimental.pallas import tpu as pltpu

F32 = jnp.float32
BF16 = jnp.bfloat16
I32 = jnp.int32

D_MODEL = 1024
RNN_BLOCK_DIM = 64
LRU_C = 8.0
HEAD_DIM = 64
N_HEADS = 16
N_EXPERTS = 32
TOP_K = 4
D_FF = 1024
SWIGLU_LIMIT = 7.0
SWIGLU_ALPHA = 1.702
RMS_EPS = 1e-6

LANES = 128
SUBLANES = 8
ROW_SLABS = D_MODEL // LANES
VMEM_LIMIT_BYTES = 56 * 1024 * 1024

TOK_TILE = 256
LRU_TILE = 256
ATT_TILE = 512
EXP_TILE = 256
ROW_TILE = 256
CONV_WIDTH = 4
GATE_CHUNK = 256


def _params(semantics):
    return pltpu.CompilerParams(dimension_semantics=semantics, vmem_limit_bytes=VMEM_LIMIT_BYTES)


def _rms(x, g):
    return x * lax.rsqrt(jnp.mean(x * x, axis=-1, keepdims=True) + RMS_EPS) * g


def _const_spec(shape):
    nd = len(shape)
    return pl.BlockSpec(shape, lambda *_: (0,) * nd)


def _inproj_body(x_ref, g_ref, w_ref, wf_ref, *outs):
    h = _rms(x_ref[...], g_ref[...]).astype(BF16)
    *wide, o_f = outs
    for j, o in enumerate(wide):
        o[...] = jnp.dot(h, w_ref[:, j * D_MODEL:(j + 1) * D_MODEL],
                         preferred_element_type=F32).astype(o.dtype)
    o_f[...] = jnp.dot(h, wf_ref[...], preferred_element_type=F32)


def _inproj(x2d, g_mix, w_wide, w_f):
    t = x2d.shape[0]
    n_wide = w_wide.shape[1] // D_MODEL
    tm = min(TOK_TILE, t)
    row = lambda i: (i, 0)
    return pl.pallas_call(
        _inproj_body,
        grid=(t // tm,),
        in_specs=[pl.BlockSpec((tm, D_MODEL), row), _const_spec((1, D_MODEL)),
                  _const_spec(w_wide.shape), _const_spec(w_f.shape)],
        out_specs=[pl.BlockSpec((tm, D_MODEL), row)] * n_wide + [pl.BlockSpec((tm, LANES), row)],
        out_shape=[jax.ShapeDtypeStruct((t, D_MODEL), BF16)] * n_wide
                  + [jax.ShapeDtypeStruct((t, LANES), F32)],
        compiler_params=_params(("arbitrary",)),
        name="inproj",
    )(x2d, g_mix.reshape(1, D_MODEL), w_wide, w_f)


def _lru_body(lx_ref, lg_ref, gl_ref, cw_ref, cb_ref, wbd_ref, brg_ref, big_ref, lam_ref, wo_ref,
              o_ref, xbuf, hcar, a_s, b_s, h_s):
    ts = lx_ref.shape[0]

    @pl.when(pl.program_id(1) == 0)
    def _():
        xbuf[0:SUBLANES, :] = jnp.zeros((SUBLANES, D_MODEL), F32)
        hcar[...] = jnp.zeros_like(hcar)

    x = lx_ref[...].astype(F32)
    xbuf[SUBLANES:SUBLANES + ts, :] = x
    xc = cb_ref[...] + cw_ref[CONV_WIDTH - 1:CONV_WIDTH, :] * x
    for k in range(CONV_WIDTH - 1):
        off = SUBLANES - (CONV_WIDTH - 1) + k
        xc = xc + cw_ref[k:k + 1, :] * xbuf[off:off + ts, :]
    xbuf[0:SUBLANES, :] = xbuf[ts:ts + SUBLANES, :]

    xcb = xc.astype(BF16)
    for c in range(D_MODEL // GATE_CHUNK):
        sl = slice(c * GATE_CHUNK, (c + 1) * GATE_CHUNK)
        g = jnp.dot(xcb[:, sl], wbd_ref[c], preferred_element_type=F32)
        rt = jax.nn.sigmoid(g[:, :GATE_CHUNK] + brg_ref[:, sl])
        it = jax.nn.sigmoid(g[:, GATE_CHUNK:] + big_ref[:, sl])
        nl = -lam_ref[:, sl]
        softplus = jnp.maximum(nl, 0.0) + jnp.log1p(jnp.exp(-jnp.abs(nl)))
        a = jnp.exp(-LRU_C * rt * softplus)
        a_s[:, sl] = a
        b_s[:, sl] = jnp.sqrt(1.0 - a * a) * (it * xc[:, sl])

    row = lax.broadcasted_iota(I32, (SUBLANES, D_MODEL), 0)

    def group(gidx, h0):
        off = pl.multiple_of(gidx * SUBLANES, SUBLANES)
        av = a_s[pl.ds(off, SUBLANES), :]
        bv = b_s[pl.ds(off, SUBLANES), :]
        for d in (1, 2, 4):
            keep = row >= d
            a_sh = jnp.where(keep, pltpu.roll(av, d, 0), 1.0)
            b_sh = jnp.where(keep, pltpu.roll(bv, d, 0), 0.0)
            bv = av * b_sh + bv
            av = av * a_sh
        hv = av * h0 + bv
        h_s[pl.ds(off, SUBLANES), :] = hv
        return jnp.broadcast_to(hv[SUBLANES - 1:SUBLANES, :], (SUBLANES, D_MODEL))

    hcar[...] = lax.fori_loop(0, ts // SUBLANES, group, hcar[...])

    y = (h_s[...] * jax.nn.gelu(lg_ref[...].astype(F32), approximate=True)).astype(BF16)
    proj = jnp.dot(y, wo_ref[...], preferred_element_type=F32)
    o_ref[...] = (jax.nn.sigmoid(gl_ref[...].astype(F32)) * proj).astype(o_ref.dtype)


def _lru(lx, lg, gl, conv_w, conv_b, wbd, b_rg, b_ig, lam, w_out, bsz, s):
    ts = min(LRU_TILE, s)
    ns = s // ts
    tile = pl.BlockSpec((ts, D_MODEL), lambda b, i: (b * ns + i, 0))
    vec = lambda a: a.reshape(1, D_MODEL)
    return pl.pallas_call(
        _lru_body,
        grid=(bsz, ns),
        in_specs=[tile, tile, tile, _const_spec((CONV_WIDTH, D_MODEL)), _const_spec((1, D_MODEL)),
                  _const_spec(wbd.shape), _const_spec((1, D_MODEL)), _const_spec((1, D_MODEL)),
                  _const_spec((1, D_MODEL)), _const_spec(w_out.shape)],
        out_specs=tile,
        out_shape=jax.ShapeDtypeStruct((bsz * s, D_MODEL), BF16),
        scratch_shapes=[pltpu.VMEM((ts + SUBLANES, D_MODEL), F32), pltpu.VMEM((SUBLANES, D_MODEL), F32),
                        pltpu.VMEM((ts, D_MODEL), F32), pltpu.VMEM((ts, D_MODEL), F32),
                        pltpu.VMEM((ts, D_MODEL), F32)],
        compiler_params=_params(("arbitrary", "arbitrary")),
        name="lru",
    )(lx, lg, gl, conv_w, vec(conv_b), wbd, vec(b_rg), vec(b_ig), vec(lam), w_out)


def _cumsum_body(f_ref, bf_ref, o_ref, carry):
    tc = f_ref.shape[0]

    @pl.when(pl.program_id(1) == 0)
    def _():
        carry[...] = jnp.zeros_like(carry)

    z = f_ref[...] + bf_ref[...]
    logf = jnp.minimum(z, 0.0) - jnp.log1p(jnp.exp(-jnp.abs(z)))
    r = lax.broadcasted_iota(I32, (tc, tc), 0)
    c = lax.broadcasted_iota(I32, (tc, tc), 1)
    tri = (r >= c).astype(F32)
    cum = jnp.dot(tri, logf, preferred_element_type=F32, precision=lax.Precision.HIGHEST)
    cum = cum + carry[0:1, :]
    o_ref[...] = cum
    carry[...] = jnp.broadcast_to(cum[tc - 1:tc, :], carry.shape)


def _cumsum(f_logit, b_f_pad, bsz, s):
    tc = min(TOK_TILE, s)
    ns = s // tc
    tile = pl.BlockSpec((tc, LANES), lambda b, i: (b * ns + i, 0))
    return pl.pallas_call(
        _cumsum_body,
        grid=(bsz, ns),
        in_specs=[tile, _const_spec((1, LANES))],
        out_specs=tile,
        out_shape=jax.ShapeDtypeStruct((bsz * s, LANES), F32),
        scratch_shapes=[pltpu.VMEM((SUBLANES, LANES), F32)],
        compiler_params=_params(("arbitrary", "arbitrary")),
        name="cumsum",
    )(f_logit, b_f_pad)


def _attn_body(q_ref, k_ref, v_ref, cq_ref, ck_ref, o_ref, m_s, l_s, acc_s, ctq_s):
    tq = q_ref.shape[0]
    tk = tq
    qi = pl.program_id(2)
    q2 = q_ref[...]
    lane = lax.broadcasted_iota(I32, (1, LANES), 1)
    head0 = lane < HEAD_DIM

    for h in range(2):
        for j in range(tq // LANES):
            blk = jnp.broadcast_to(cq_ref[h:h + 1, j * LANES:(j + 1) * LANES], (LANES, LANES))
            ctq_s[h, j * LANES:(j + 1) * LANES, :] = blk.T
    m_s[...] = jnp.full_like(m_s, -jnp.inf)
    l_s[...] = jnp.zeros_like(l_s)
    acc_s[...] = jnp.zeros_like(acc_s)

    def step(ki, masked):
        off = pl.multiple_of(ki * tk, tk)
        k2 = k_ref[pl.ds(off, tk), :]
        v2 = v_ref[pl.ds(off, tk), :]
        alphas, pvs = [], []
        for h in range(2):
            hs = slice(h * HEAD_DIM, (h + 1) * HEAD_DIM)
            sc = lax.dot_general(q2[:, hs], k2[:, hs], (((1,), (1,)), ((), ())),
                                 preferred_element_type=F32)
            ck = ck_ref[h:h + 1, pl.ds(off, tk)]
            sc = sc + (ctq_s[h, :, 0:1] - ck)
            if masked:
                r = lax.broadcasted_iota(I32, (tq, tk), 0)
                c = lax.broadcasted_iota(I32, (tq, tk), 1)
                sc = jnp.where(c <= r, sc, -jnp.inf)
            m_prev = m_s[h]
            m_new = jnp.maximum(m_prev, jnp.max(sc, axis=1, keepdims=True))
            alpha = jnp.exp(m_prev - m_new)
            p = jnp.exp(sc - m_new)
            l_s[h] = alpha * l_s[h] + jnp.sum(p, axis=1, keepdims=True)
            m_s[h] = m_new
            vh = jnp.where(head0 if h == 0 else jnp.logical_not(head0), v2, jnp.zeros_like(v2))
            pvs.append(jnp.dot(p.astype(BF16), vh, preferred_element_type=F32))
            alphas.append(alpha)
        alpha2 = jnp.where(head0, alphas[0], alphas[1])
        acc_s[...] = alpha2 * acc_s[...] + pvs[0] + pvs[1]

    def full_step(ki, carry):
        step(ki, False)
        return carry

    lax.fori_loop(0, qi, full_step, 0)
    step(qi, True)
    l2 = jnp.where(head0, l_s[0], l_s[1])
    o_ref[...] = (acc_s[...] / l2).astype(o_ref.dtype)


def _attention(q, k, v, ct, bsz, s):
    tq = min(ATT_TILE, s)
    nq = s // tq
    npair = N_HEADS // 2
    qspec = pl.BlockSpec((tq, LANES), lambda b, p, i: (b * nq + i, p))
    kvspec = pl.BlockSpec((s, LANES), lambda b, p, i: (b, p))
    return pl.pallas_call(
        _attn_body,
        grid=(bsz, npair, nq),
        in_specs=[qspec, kvspec, kvspec,
                  pl.BlockSpec((None, None, 2, tq), lambda b, p, i: (b, p, 0, i)),
                  pl.BlockSpec((None, None, 2, s), lambda b, p, i: (b, p, 0, 0))],
        out_specs=qspec,
        out_shape=jax.ShapeDtypeStruct((bsz * s, D_MODEL), BF16),
        scratch_shapes=[pltpu.VMEM((2, tq, 1), F32), pltpu.VMEM((2, tq, 1), F32),
                        pltpu.VMEM((tq, LANES), F32), pltpu.VMEM((2, tq, LANES), F32)],
        compiler_params=_params(("arbitrary", "arbitrary", "arbitrary")),
        name="attn",
    )(q, k, v, ct, ct)


def _merge_body(x_ref, ml_ref, yf_ref, gf_ref, wfo_ref, wo_ref, g_ref, wr_ref, br_ref,
                x2_ref, h2_ref, mi_ref, mw_ref, cnt_ref, running):
    tm = x_ref.shape[0]

    @pl.when(pl.program_id(0) == 0)
    def _():
        running[...] = jnp.zeros_like(running)

    fox = jnp.dot(yf_ref[...], wfo_ref[...], preferred_element_type=F32)
    merged = ml_ref[...].astype(F32) + jax.nn.sigmoid(gf_ref[...].astype(F32)) * fox
    x2 = x_ref[...] + jnp.dot(merged.astype(BF16), wo_ref[...], preferred_element_type=F32)
    x2_ref[...] = x2
    h2 = _rms(x2, g_ref[...])
    for c in range(ROW_SLABS):
        h2_ref[pl.ds(c, tm, stride=ROW_SLABS), :] = h2[:, c * LANES:(c + 1) * LANES]

    logits = jnp.dot(h2, wr_ref[...], preferred_element_type=F32,
                     precision=lax.Precision.HIGHEST) + br_ref[...]
    lane = lax.broadcasted_iota(I32, (tm, LANES), 1)
    logits = jnp.where(lane < N_EXPERTS, logits, -jnp.inf)
    vals, idxs = [], []
    for _ in range(TOP_K):
        mx = jnp.max(logits, axis=1, keepdims=True)
        ix = jnp.min(jnp.where(logits == mx, lane, LANES), axis=1, keepdims=True)
        vals.append(mx)
        idxs.append(ix)
        logits = jnp.where(lane == ix, -jnp.inf, logits)
    exps = [jnp.exp(vk - vals[0]) for vk in vals]
    denom = exps[0] + exps[1] + exps[2] + exps[3]

    onehot = jnp.zeros((tm, LANES), F32)
    for ix in idxs:
        onehot = onehot + (lane == ix).astype(F32)
    r = lax.broadcasted_iota(I32, (tm, tm), 0)
    c = lax.broadcasted_iota(I32, (tm, tm), 1)
    before = jnp.dot((c < r).astype(BF16), onehot.astype(BF16), preferred_element_type=F32)
    pos = before + running[0:1, :]
    mi = jnp.zeros((tm, LANES), I32)
    mw = jnp.zeros((tm, LANES), F32)
    for kk in range(TOP_K):
        rank = jnp.sum(jnp.where(lane == idxs[kk], pos, 0.0), axis=1, keepdims=True).astype(I32)
        mi = jnp.where(lane == kk, idxs[kk], mi)
        mi = jnp.where(lane == TOP_K + kk, rank, mi)
        mw = jnp.where(lane == kk, exps[kk] / denom, mw)
    mi_ref[...] = mi
    mw_ref[...] = mw
    total = running[0:1, :] + jnp.sum(onehot, axis=0, keepdims=True)
    running[...] = jnp.broadcast_to(total, running.shape)
    cnt_ref[...] = jnp.broadcast_to(total, cnt_ref.shape).astype(I32)


def _merge(x2d, ml, yf, gf, w_fo, w_o, g_ffn, w_r, b_r):
    t = x2d.shape[0]
    tm = min(TOK_TILE, t)
    row = lambda i: (i, 0)
    tile = pl.BlockSpec((tm, D_MODEL), row)
    meta = pl.BlockSpec((tm, LANES), row)
    return pl.pallas_call(
        _merge_body,
        grid=(t // tm,),
        in_specs=[tile, tile, tile, tile, _const_spec(w_fo.shape), _const_spec(w_o.shape),
                  _const_spec((1, D_MODEL)), _const_spec(w_r.shape), _const_spec((1, LANES))],
        out_specs=[tile, pl.BlockSpec((tm * ROW_SLABS, LANES), row), meta, meta,
                   _const_spec((SUBLANES, LANES))],
        out_shape=[jax.ShapeDtypeStruct((t, D_MODEL), F32),
                   jax.ShapeDtypeStruct((t * ROW_SLABS, LANES), F32),
                   jax.ShapeDtypeStruct((t, LANES), I32),
                   jax.ShapeDtypeStruct((t, LANES), F32),
                   jax.ShapeDtypeStruct((SUBLANES, LANES), I32)],
        scratch_shapes=[pltpu.VMEM((SUBLANES, LANES), F32)],
        compiler_params=_params(("arbitrary",)),
        name="merge",
    )(x2d, ml, yf, gf, w_fo, w_o, g_ffn.reshape(1, D_MODEL), w_r, b_r)


def _row_slab(ref, row):
    return ref.at[pl.ds(pl.multiple_of(row * ROW_SLABS, ROW_SLABS), ROW_SLABS), :]


def _scatter_body(dest_ref, h2_hbm, xs_in_hbm, xs_hbm, sem):
    del xs_in_hbm
    tb = dest_ref.shape[-1] // TOP_K
    base = pl.program_id(0) * tb

    def issue(j, carry):
        for kk in range(TOP_K):
            d = dest_ref[0, 0, j * TOP_K + kk]
            pltpu.make_async_copy(_row_slab(h2_hbm, base + j), _row_slab(xs_hbm, d), sem).start()
        return carry

    lax.fori_loop(0, tb, issue, 0)
    n = tb * TOP_K * ROW_SLABS
    pltpu.make_async_copy(xs_hbm.at[pl.ds(0, n), :], xs_hbm.at[pl.ds(0, n), :], sem).wait()


def _scatter(dest, h2s, n_rows):
    t = dest.shape[0]
    tb = min(ROW_TILE, t)
    nb = t // tb
    dest3 = dest.reshape(nb, 1, tb * TOP_K)
    zeros = jnp.zeros((n_rows * ROW_SLABS, LANES), F32)
    return pl.pallas_call(
        _scatter_body,
        grid=(nb,),
        in_specs=[pl.BlockSpec((1, 1, tb * TOP_K), lambda i: (i, 0, 0), memory_space=pltpu.SMEM),
                  pl.BlockSpec(memory_space=pl.ANY), pl.BlockSpec(memory_space=pl.ANY)],
        out_specs=pl.BlockSpec(memory_space=pl.ANY),
        out_shape=jax.ShapeDtypeStruct((n_rows * ROW_SLABS, LANES), F32),
        scratch_shapes=[pltpu.SemaphoreType.DMA(())],
        input_output_aliases={2: 0},
        compiler_params=_params(("arbitrary",)),
        name="scatter",
    )(dest3, h2s, zeros)


def _expert_body(te_ref, nu_ref, xs_ref, wgu_ref, bgu_ref, wd_ref, bd_ref, ys_ref, wgu_bf, wd_bf):
    i = pl.program_id(0)
    tm = xs_ref.shape[0] // ROW_SLABS
    e = te_ref[i]
    e_prev = te_ref[jnp.maximum(i - 1, 0)]

    @pl.when(jnp.logical_or(i == 0, e != e_prev))
    def _():
        wgu_bf[...] = wgu_ref[...].astype(BF16)
        wd_bf[...] = wd_ref[...].astype(BF16)

    @pl.when(i < nu_ref[0])
    def _():
        x = jnp.concatenate([xs_ref[pl.ds(c, tm, stride=ROW_SLABS), :] for c in range(ROW_SLABS)],
                            axis=1).astype(BF16)
        gu = jnp.dot(x, wgu_bf[...], preferred_element_type=F32) + bgu_ref[...]
        gate = jnp.minimum(gu[:, :D_FF], SWIGLU_LIMIT)
        up = jnp.clip(gu[:, D_FF:], -SWIGLU_LIMIT, SWIGLU_LIMIT)
        glu = gate * jax.nn.sigmoid(SWIGLU_ALPHA * gate)
        y = jnp.dot(((up + 1.0) * glu).astype(BF16), wd_bf[...],
                    preferred_element_type=F32) + bd_ref[...]
        for c in range(ROW_SLABS):
            ys_ref[pl.ds(c, tm, stride=ROW_SLABS), :] = y[:, c * LANES:(c + 1) * LANES]


def _experts(tile_expert, n_used, xs, w_gate_up, b_gate_up, w_down, b_down, n_tiles):
    tm = EXP_TILE
    rows = lambda i, te, nu: (jnp.minimum(i, nu[0] - 1), 0)
    wsel = lambda i, te, nu: (te[i], 0, 0)
    grid_spec = pltpu.PrefetchScalarGridSpec(
        num_scalar_prefetch=2,
        grid=(n_tiles,),
        in_specs=[pl.BlockSpec((tm * ROW_SLABS, LANES), rows),
                  pl.BlockSpec((None, D_MODEL, 2 * D_FF), wsel),
                  pl.BlockSpec((None, 1, 2 * D_FF), wsel),
                  pl.BlockSpec((None, D_FF, D_MODEL), wsel),
                  pl.BlockSpec((None, 1, D_MODEL), wsel)],
        out_specs=pl.BlockSpec((tm * ROW_SLABS, LANES), rows),
        scratch_shapes=[pltpu.VMEM((D_MODEL, 2 * D_FF), BF16), pltpu.VMEM((D_FF, D_MODEL), BF16)],
    )
    return pl.pallas_call(
        _expert_body,
        grid_spec=grid_spec,
        out_shape=jax.ShapeDtypeStruct(xs.shape, F32),
        input_output_aliases={2: 0},
        compiler_params=_params(("arbitrary",)),
        name="experts",
    )(tile_expert, n_used, xs, w_gate_up, b_gate_up.reshape(N_EXPERTS, 1, 2 * D_FF),
      w_down, b_down.reshape(N_EXPERTS, 1, D_MODEL))


def _combine_body(dest_ref, x2_ref, mw_ref, g_ref, ys_hbm, o_ref, buf, sem):
    tb = x2_ref.shape[0]

    def issue(j, carry):
        for kk in range(TOP_K):
            d = dest_ref[0, 0, j * TOP_K + kk]
            pltpu.make_async_copy(_row_slab(ys_hbm, d), _row_slab(buf, kk * tb + j), sem).start()
        return carry

    lax.fori_loop(0, tb, issue, 0)
    pltpu.make_async_copy(buf, buf, sem).wait()

    mw = mw_ref[...]
    cols = []
    for c in range(ROW_SLABS):
        acc = x2_ref[:, c * LANES:(c + 1) * LANES]
        for kk in range(TOP_K):
            rows = buf[pl.ds(kk * tb * ROW_SLABS + c, tb, stride=ROW_SLABS), :]
            acc = acc + mw[:, kk:kk + 1] * rows
        cols.append(acc)
    o_ref[...] = _rms(jnp.concatenate(cols, axis=1), g_ref[...])


def _combine(dest, x2, mw, g_final, ys):
    t = x2.shape[0]
    tb = min(ROW_TILE, t)
    nb = t // tb
    dest3 = dest.reshape(nb, 1, tb * TOP_K)
    row = lambda i: (i, 0)
    return pl.pallas_call(
        _combine_body,
        grid=(nb,),
        in_specs=[pl.BlockSpec((1, 1, tb * TOP_K), lambda i: (i, 0, 0), memory_space=pltpu.SMEM),
                  pl.BlockSpec((tb, D_MODEL), row), pl.BlockSpec((tb, LANES), row),
                  _const_spec((1, D_MODEL)), pl.BlockSpec(memory_space=pl.ANY)],
        out_specs=pl.BlockSpec((tb, D_MODEL), row),
        out_shape=jax.ShapeDtypeStruct((t, D_MODEL), F32),
        scratch_shapes=[pltpu.VMEM((TOP_K * tb * ROW_SLABS, LANES), F32), pltpu.SemaphoreType.DMA(())],
        compiler_params=_params(("arbitrary",)),
        name="combine",
    )(dest3, x2, mw, g_final.reshape(1, D_MODEL), ys)


def _block_diag_gates(w_rg, w_ig):
    per = GATE_CHUNK // RNN_BLOCK_DIM
    nchunk = w_rg.shape[0] // per
    eye = jnp.eye(per, dtype=w_rg.dtype)

    def bd(w):
        w = w.reshape(nchunk, per, RNN_BLOCK_DIM, RNN_BLOCK_DIM)
        return jnp.einsum('cpij,pq->cpiqj', w, eye).reshape(nchunk, GATE_CHUNK, GATE_CHUNK)

    return jnp.concatenate([bd(w_rg), bd(w_ig)], axis=-1).astype(BF16)


def kernel(x, g_mix, w_in, conv_w, conv_b, w_rg, b_rg, w_ig, b_ig, lru_lambda, b_f, w_lru_out,
           w_fox_out, w_o, g_ffn, w_router, b_router, w_gate_up, b_gate_up, w_down, b_down, g_final):
    bsz, s, d = x.shape
    assert d == D_MODEL
    t = bsz * s
    x2d = x.reshape(t, d)

    o_q = 2 * D_MODEL
    o_f = 5 * D_MODEL
    o_g = o_f + N_HEADS
    scale = 1.0 / math.sqrt(HEAD_DIM)
    w_wide = jnp.concatenate(
        [w_in[:, :o_q], w_in[:, o_q:o_q + D_MODEL] * scale, w_in[:, o_q + D_MODEL:o_f], w_in[:, o_g:]],
        axis=1).astype(BF16)
    w_f = jnp.pad(w_in[:, o_f:o_g], ((0, 0), (0, LANES - N_HEADS))).astype(BF16)
    b_f_pad = jnp.pad(b_f, (0, LANES - N_HEADS)).reshape(1, LANES)
    w_r = jnp.pad(w_router, ((0, 0), (0, LANES - N_EXPERTS)))
    b_r = jnp.pad(b_router, (0, LANES - N_EXPERTS)).reshape(1, LANES)

    lx, lg, q, k, v, gl, gf, f_logit = _inproj(x2d, g_mix, w_wide, w_f)

    ml = _lru(lx, lg, gl, conv_w, conv_b, _block_diag_gates(w_rg, w_ig), b_rg, b_ig, lru_lambda,
              w_lru_out.astype(BF16), bsz, s)

    cum = _cumsum(f_logit, b_f_pad, bsz, s)
    ct = cum[:, :N_HEADS].reshape(bsz, s, N_HEADS // 2, 2).transpose(0, 2, 3, 1)
    yf = _attention(q, k, v, ct, bsz, s)

    x2, h2s, mi, mw, cnt = _merge(x2d, ml, yf, gf, w_fox_out.astype(BF16), w_o.astype(BF16),
                                  g_ffn, w_r, b_r)

    idx = mi[:, :TOP_K]
    rank = mi[:, TOP_K:2 * TOP_K]
    counts = cnt[0, :N_EXPERTS]
    tiles_e = (counts + EXP_TILE - 1) // EXP_TILE
    tile_end = jnp.cumsum(tiles_e)
    row_off = (tile_end - tiles_e) * EXP_TILE
    dest = (row_off[idx] + rank).astype(I32)
    n_tiles = (t * TOP_K) // EXP_TILE + N_EXPERTS
    n_used = tile_end[-1:].astype(I32)
    tile_ids = jnp.minimum(jnp.arange(n_tiles, dtype=I32), n_used[0] - 1)
    tile_expert = jnp.sum(tile_ids[:, None] >= tile_end[None, :], axis=1).astype(I32)

    xs = _scatter(dest, h2s, n_tiles * EXP_TILE)
    ys = _experts(tile_expert, n_used, xs, w_gate_up, b_gate_up, w_down, b_down, n_tiles)
    out = _combine(dest, x2, mw, g_final, ys)
    return out.reshape(bsz, s, d)
```

```python
import functools
import math

import jax
import jax.numpy as jnp
from jax import lax
from jax.experimental import pallas as pl
from jax.experimental.pallas import tpu as pltpu

F32 = jnp.float32
BF16 = jnp.bfloat16
I32 = jnp.int32

D_MODEL = 1024
RNN_BLOCK_DIM = 64
LRU_C = 8.0
HEAD_DIM = 64
N_HEADS = 16
N_EXPERTS = 32
TOP_K = 4
D_FF = 1024
SWIGLU_LIMIT = 7.0
SWIGLU_ALPHA = 1.702
RMS_EPS = 1e-6

LANES = 128
SUBLANES = 8
ROW_SLABS = D_MODEL // LANES
VMEM_LIMIT_BYTES = 56 * 1024 * 1024

TOK_TILE = 256
LRU_TILE = 256
ATT_TILE = 512
EXP_TILE = 256
ROW_TILE = 256
CONV_WIDTH = 4
GATE_CHUNK = 256


def _params(semantics):
    return pltpu.CompilerParams(dimension_semantics=semantics, vmem_limit_bytes=VMEM_LIMIT_BYTES)


def _rms(x, g):
    return x * lax.rsqrt(jnp.mean(x * x, axis=-1, keepdims=True) + RMS_EPS) * g


def _const_spec(shape):
    nd = len(shape)
    return pl.BlockSpec(shape, lambda *_: (0,) * nd)


def _inproj_body(x_ref, g_ref, w_ref, wf_ref, *outs):
    h = _rms(x_ref[...], g_ref[...]).astype(BF16)
    *wide, o_f = outs
    for j, o in enumerate(wide):
        o[...] = jnp.dot(h, w_ref[:, j * D_MODEL:(j + 1) * D_MODEL],
                         preferred_element_type=F32).astype(o.dtype)
    o_f[...] = jnp.dot(h, wf_ref[...], preferred_element_type=F32)


def _inproj(x2d, g_mix, w_wide, w_f):
    t = x2d.shape[0]
    n_wide = w_wide.shape[1] // D_MODEL
    tm = min(TOK_TILE, t)
    row = lambda i: (i, 0)
    return pl.pallas_call(
        _inproj_body,
        grid=(t // tm,),
        in_specs=[pl.BlockSpec((tm, D_MODEL), row), _const_spec((1, D_MODEL)),
                  _const_spec(w_wide.shape), _const_spec(w_f.shape)],
        out_specs=[pl.BlockSpec((tm, D_MODEL), row)] * n_wide + [pl.BlockSpec((tm, LANES), row)],
        out_shape=[jax.ShapeDtypeStruct((t, D_MODEL), BF16)] * n_wide
                  + [jax.ShapeDtypeStruct((t, LANES), F32)],
        compiler_params=_params(("arbitrary",)),
        name="inproj",
    )(x2d, g_mix.reshape(1, D_MODEL), w_wide, w_f)


def _lru_body(lx_ref, lg_ref, gl_ref, cw_ref, cb_ref, wbd_ref, brg_ref, big_ref, lam_ref, wo_ref,
              o_ref, xbuf, hcar, a_s, b_s, h_s):
    ts = lx_ref.shape[0]

    @pl.when(pl.program_id(1) == 0)
    def _():
        xbuf[0:SUBLANES, :] = jnp.zeros((SUBLANES, D_MODEL), F32)
        hcar[...] = jnp.zeros_like(hcar)

    x = lx_ref[...].astype(F32)
    xbuf[SUBLANES:SUBLANES + ts, :] = x
    xc = cb_ref[...] + cw_ref[CONV_WIDTH - 1:CONV_WIDTH, :] * x
    for k in range(CONV_WIDTH - 1):
        off = SUBLANES - (CONV_WIDTH - 1) + k
        xc = xc + cw_ref[k:k + 1, :] * xbuf[off:off + ts, :]
    xbuf[0:SUBLANES, :] = xbuf[ts:ts + SUBLANES, :]

    xcb = xc.astype(BF16)
    for c in range(D_MODEL // GATE_CHUNK):
        sl = slice(c * GATE_CHUNK, (c + 1) * GATE_CHUNK)
        g = jnp.dot(xcb[:, sl], wbd_ref[c], preferred_element_type=F32)
        rt = jax.nn.sigmoid(g[:, :GATE_CHUNK] + brg_ref[:, sl])
        it = jax.nn.sigmoid(g[:, GATE_CHUNK:] + big_ref[:, sl])
        nl = -lam_ref[:, sl]
        softplus = jnp.maximum(nl, 0.0) + jnp.log1p(jnp.exp(-jnp.abs(nl)))
        a = jnp.exp(-LRU_C * rt * softplus)
        a_s[:, sl] = a
        b_s[:, sl] = jnp.sqrt(1.0 - a * a) * (it * xc[:, sl])

    row = lax.broadcasted_iota(I32, (SUBLANES, D_MODEL), 0)

    def group(gidx, h0):
        off = pl.multiple_of(gidx * SUBLANES, SUBLANES)
        av = a_s[pl.ds(off, SUBLANES), :]
        bv = b_s[pl.ds(off, SUBLANES), :]
        for d in (1, 2, 4):
            keep = row >= d
            a_sh = jnp.where(keep, pltpu.roll(av, d, 0), 1.0)
            b_sh = jnp.where(keep, pltpu.roll(bv, d, 0), 0.0)
            bv = av * b_sh + bv
            av = av * a_sh
        hv = av * h0 + bv
        h_s[pl.ds(off, SUBLANES), :] = hv
        return jnp.broadcast_to(hv[SUBLANES - 1:SUBLANES, :], (SUBLANES, D_MODEL))

    hcar[...] = lax.fori_loop(0, ts // SUBLANES, group, hcar[...])

    y = (h_s[...] * jax.nn.gelu(lg_ref[...].astype(F32), approximate=True)).astype(BF16)
    proj = jnp.dot(y, wo_ref[...], preferred_element_type=F32)
    o_ref[...] = (jax.nn.sigmoid(gl_ref[...].astype(F32)) * proj).astype(o_ref.dtype)


def _lru(lx, lg, gl, conv_w, conv_b, wbd, b_rg, b_ig, lam, w_out, bsz, s):
    ts = min(LRU_TILE, s)
    ns = s // ts
    tile = pl.BlockSpec((ts, D_MODEL), lambda b, i: (b * ns + i, 0))
    vec = lambda a: a.reshape(1, D_MODEL)
    return pl.pallas_call(
        _lru_body,
        grid=(bsz, ns),
        in_specs=[tile, tile, tile, _const_spec((CONV_WIDTH, D_MODEL)), _const_spec((1, D_MODEL)),
                  _const_spec(wbd.shape), _const_spec((1, D_MODEL)), _const_spec((1, D_MODEL)),
                  _const_spec((1, D_MODEL)), _const_spec(w_out.shape)],
        out_specs=tile,
        out_shape=jax.ShapeDtypeStruct((bsz * s, D_MODEL), BF16),
        scratch_shapes=[pltpu.VMEM((ts + SUBLANES, D_MODEL), F32), pltpu.VMEM((SUBLANES, D_MODEL), F32),
                        pltpu.VMEM((ts, D_MODEL), F32), pltpu.VMEM((ts, D_MODEL), F32),
                        pltpu.VMEM((ts, D_MODEL), F32)],
        compiler_params=_params(("arbitrary", "arbitrary")),
        name="lru",
    )(lx, lg, gl, conv_w, vec(conv_b), wbd, vec(b_rg), vec(b_ig), vec(lam), w_out)


def _cumsum_body(f_ref, bf_ref, o_ref, carry):
    tc = f_ref.shape[0]

    @pl.when(pl.program_id(1) == 0)
    def _():
        carry[...] = jnp.zeros_like(carry)

    z = f_ref[...] + bf_ref[...]
    logf = jnp.minimum(z, 0.0) - jnp.log1p(jnp.exp(-jnp.abs(z)))
    r = lax.broadcasted_iota(I32, (tc, tc), 0)
    c = lax.broadcasted_iota(I32, (tc, tc), 1)
    tri = (r >= c).astype(F32)
    cum = jnp.dot(tri, logf, preferred_element_type=F32, precision=lax.Precision.HIGHEST)
    cum = cum + carry[0:1, :]
    o_ref[...] = cum
    carry[...] = jnp.broadcast_to(cum[tc - 1:tc, :], carry.shape)


def _cumsum(f_logit, b_f_pad, bsz, s):
    tc = min(TOK_TILE, s)
    ns = s // tc
    tile = pl.BlockSpec((tc, LANES), lambda b, i: (b * ns + i, 0))
    return pl.pallas_call(
        _cumsum_body,
        grid=(bsz, ns),
        in_specs=[tile, _const_spec((1, LANES))],
        out_specs=tile,
        out_shape=jax.ShapeDtypeStruct((bsz * s, LANES), F32),
        scratch_shapes=[pltpu.VMEM((SUBLANES, LANES), F32)],
        compiler_params=_params(("arbitrary", "arbitrary")),
        name="cumsum",
    )(f_logit, b_f_pad)


def _attn_body(q_ref, k_ref, v_ref, cq_ref, ck_ref, o_ref, m_s, l_s, acc_s, ctq_s):
    tq = q_ref.shape[0]
    tk = tq
    qi = pl.program_id(2)
    q2 = q_ref[...]
    lane = lax.broadcasted_iota(I32, (1, LANES), 1)
    head0 = lane < HEAD_DIM

    for h in range(2):
        for j in range(tq // LANES):
            blk = jnp.broadcast_to(cq_ref[h:h + 1, j * LANES:(j + 1) * LANES], (LANES, LANES))
            ctq_s[h, j * LANES:(j + 1) * LANES, :] = blk.T
    m_s[...] = jnp.full_like(m_s, -jnp.inf)
    l_s[...] = jnp.zeros_like(l_s)
    acc_s[...] = jnp.zeros_like(acc_s)

    def step(ki, masked):
        off = pl.multiple_of(ki * tk, tk)
        k2 = k_ref[pl.ds(off, tk), :]
        v2 = v_ref[pl.ds(off, tk), :]
        alphas, pvs = [], []
        for h in range(2):
            hs = slice(h * HEAD_DIM, (h + 1) * HEAD_DIM)
            sc = lax.dot_general(q2[:, hs], k2[:, hs], (((1,), (1,)), ((), ())),
                                 preferred_element_type=F32)
            ck = ck_ref[h:h + 1, pl.ds(off, tk)]
            sc = sc + (ctq_s[h, :, 0:1] - ck)
            if masked:
                r = lax.broadcasted_iota(I32, (tq, tk), 0)
                c = lax.broadcasted_iota(I32, (tq, tk), 1)
                sc = jnp.where(c <= r, sc, -jnp.inf)
            m_prev = m_s[h]
            m_new = jnp.maximum(m_prev, jnp.max(sc, axis=1, keepdims=True))
            alpha = jnp.exp(m_prev - m_new)
            p = jnp.exp(sc - m_new)
            l_s[h] = alpha * l_s[h] + jnp.sum(p, axis=1, keepdims=True)
            m_s[h] = m_new
            vh = jnp.where(head0 if h == 0 else jnp.logical_not(head0), v2, jnp.zeros_like(v2))
            pvs.append(jnp.dot(p.astype(BF16), vh, preferred_element_type=F32))
            alphas.append(alpha)
        alpha2 = jnp.where(head0, alphas[0], alphas[1])
        acc_s[...] = alpha2 * acc_s[...] + pvs[0] + pvs[1]

    def full_step(ki, carry):
        step(ki, False)
        return carry

    lax.fori_loop(0, qi, full_step, 0)
    step(qi, True)
    l2 = jnp.where(head0, l_s[0], l_s[1])
    o_ref[...] = (acc_s[...] / l2).astype(o_ref.dtype)


def _attention(q, k, v, ct, bsz, s):
    tq = min(ATT_TILE, s)
    nq = s // tq
    npair = N_HEADS // 2
    qspec = pl.BlockSpec((tq, LANES), lambda b, p, i: (b * nq + i, p))
    kvspec = pl.BlockSpec((s, LANES), lambda b, p, i: (b, p))
    return pl.pallas_call(
        _attn_body,
        grid=(bsz, npair, nq),
        in_specs=[qspec, kvspec, kvspec,
                  pl.BlockSpec((None, None, 2, tq), lambda b, p, i: (b, p, 0, i)),
                  pl.BlockSpec((None, None, 2, s), lambda b, p, i: (b, p, 0, 0))],
        out_specs=qspec,
        out_shape=jax.ShapeDtypeStruct((bsz * s, D_MODEL), BF16),
        scratch_shapes=[pltpu.VMEM((2, tq, 1), F32), pltpu.VMEM((2, tq, 1), F32),
                        pltpu.VMEM((tq, LANES), F32), pltpu.VMEM((2, tq, LANES), F32)],
        compiler_params=_params(("arbitrary", "arbitrary", "arbitrary")),
        name="attn",
    )(q, k, v, ct, ct)


def _merge_body(x_ref, ml_ref, yf_ref, gf_ref, wfo_ref, wo_ref, g_ref, wr_ref, br_ref,
                x2_ref, h2_ref, mi_ref, mw_ref, cnt_ref, running):
    tm = x_ref.shape[0]

    @pl.when(pl.program_id(0) == 0)
    def _():
        running[...] = jnp.zeros_like(running)

    fox = jnp.dot(yf_ref[...], wfo_ref[...], preferred_element_type=F32)
    merged = ml_ref[...].astype(F32) + jax.nn.sigmoid(gf_ref[...].astype(F32)) * fox
    x2 = x_ref[...] + jnp.dot(merged.astype(BF16), wo_ref[...], preferred_element_type=F32)
    x2_ref[...] = x2
    h2 = _rms(x2, g_ref[...])
    for c in range(ROW_SLABS):
        h2_ref[pl.ds(c, tm, stride=ROW_SLABS), :] = h2[:, c * LANES:(c + 1) * LANES]

    logits = jnp.dot(h2, wr_ref[...], preferred_element_type=F32,
                     precision=lax.Precision.HIGHEST) + br_ref[...]
    lane = lax.broadcasted_iota(I32, (tm, LANES), 1)
    logits = jnp.where(lane < N_EXPERTS, logits, -jnp.inf)
    vals, idxs = [], []
    for _ in range(TOP_K):
        mx = jnp.max(logits, axis=1, keepdims=True)
        ix = jnp.min(jnp.where(logits == mx, lane, LANES), axis=1, keepdims=True)
        vals.append(mx)
        idxs.append(ix)
        logits = jnp.where(lane == ix, -jnp.inf, logits)
    exps = [jnp.exp(vk - vals[0]) for vk in vals]
    denom = exps[0] + exps[1] + exps[2] + exps[3]

    onehot = jnp.zeros((tm, LANES), F32)
    for ix in idxs:
        onehot = onehot + (lane == ix).astype(F32)
    r = lax.broadcasted_iota(I32, (tm, tm), 0)
    c = lax.broadcasted_iota(I32, (tm, tm), 1)
    before = jnp.dot((c < r).astype(BF16), onehot.astype(BF16), preferred_element_type=F32)
    pos = before + running[0:1, :]
    mi = jnp.zeros((tm, LANES), I32)
    mw = jnp.zeros((tm, LANES), F32)
    for kk in range(TOP_K):
        rank = jnp.sum(jnp.where(lane == idxs[kk], pos, 0.0), axis=1, keepdims=True).astype(I32)
        mi = jnp.where(lane == kk, idxs[kk], mi)
        mi = jnp.where(lane == TOP_K + kk, rank, mi)
        mw = jnp.where(lane == kk, exps[kk] / denom, mw)
    mi_ref[...] = mi
    mw_ref[...] = mw
    total = running[0:1, :] + jnp.sum(onehot, axis=0, keepdims=True)
    running[...] = jnp.broadcast_to(total, running.shape)
    cnt_ref[...] = jnp.broadcast_to(total, cnt_ref.shape).astype(I32)


def _merge(x2d, ml, yf, gf, w_fo, w_o, g_ffn, w_r, b_r):
    t = x2d.shape[0]
    tm = min(TOK_TILE, t)
    row = lambda i: (i, 0)
    tile = pl.BlockSpec((tm, D_MODEL), row)
    meta = pl.BlockSpec((tm, LANES), row)
    return pl.pallas_call(
        _merge_body,
        grid=(t // tm,),
        in_specs=[tile, tile, tile, tile, _const_spec(w_fo.shape), _const_spec(w_o.shape),
                  _const_spec((1, D_MODEL)), _const_spec(w_r.shape), _const_spec((1, LANES))],
        out_specs=[tile, pl.BlockSpec((tm * ROW_SLABS, LANES), row), meta, meta,
                   _const_spec((SUBLANES, LANES))],
        out_shape=[jax.ShapeDtypeStruct((t, D_MODEL), F32),
                   jax.ShapeDtypeStruct((t * ROW_SLABS, LANES), F32),
                   jax.ShapeDtypeStruct((t, LANES), I32),
                   jax.ShapeDtypeStruct((t, LANES), F32),
                   jax.ShapeDtypeStruct((SUBLANES, LANES), I32)],
        scratch_shapes=[pltpu.VMEM((SUBLANES, LANES), F32)],
        compiler_params=_params(("arbitrary",)),
        name="merge",
    )(x2d, ml, yf, gf, w_fo, w_o, g_ffn.reshape(1, D_MODEL), w_r, b_r)


def _row_slab(ref, row):
    return ref.at[pl.ds(pl.multiple_of(row * ROW_SLABS, ROW_SLABS), ROW_SLABS), :]


def _scatter_body(dest_ref, h2_ref, xs_in_hbm, xs_hbm, sem):
    del xs_in_hbm
    tb = dest_ref.shape[-1] // TOP_K

    def issue(j, carry):
        for kk in range(TOP_K):
            d = dest_ref[0, 0, j * TOP_K + kk]
            pltpu.make_async_copy(_row_slab(h2_ref, j), _row_slab(xs_hbm, d), sem).start()
        return carry

    lax.fori_loop(0, tb, issue, 0)
    n = tb * TOP_K * ROW_SLABS
    pltpu.make_async_copy(xs_hbm.at[pl.ds(0, n), :], xs_hbm.at[pl.ds(0, n), :], sem).wait()


def _scatter(dest, h2s, n_rows):
    t = dest.shape[0]
    tb = min(ROW_TILE, t)
    nb = t // tb
    dest3 = dest.reshape(nb, 1, tb * TOP_K)
    zeros = jnp.zeros((n_rows * ROW_SLABS, LANES), F32)
    return pl.pallas_call(
        _scatter_body,
        grid=(nb,),
        in_specs=[pl.BlockSpec((1, 1, tb * TOP_K), lambda i: (i, 0, 0), memory_space=pltpu.SMEM),
                  pl.BlockSpec((tb * ROW_SLABS, LANES), lambda i: (i, 0)),
                  pl.BlockSpec(memory_space=pl.ANY)],
        out_specs=pl.BlockSpec(memory_space=pl.ANY),
        out_shape=jax.ShapeDtypeStruct((n_rows * ROW_SLABS, LANES), F32),
        scratch_shapes=[pltpu.SemaphoreType.DMA(())],
        input_output_aliases={2: 0},
        compiler_params=_params(("arbitrary",)),
        name="scatter",
    )(dest3, h2s, zeros)


def _expert_body(te_ref, nu_ref, xs_ref, wgu_ref, bgu_ref, wd_ref, bd_ref, ys_ref, wgu_bf, wd_bf):
    i = pl.program_id(0)
    tm = xs_ref.shape[0] // ROW_SLABS
    e = te_ref[i]
    e_prev = te_ref[jnp.maximum(i - 1, 0)]

    @pl.when(jnp.logical_or(i == 0, e != e_prev))
    def _():
        wgu_bf[...] = wgu_ref[...].astype(BF16)
        wd_bf[...] = wd_ref[...].astype(BF16)

    @pl.when(i < nu_ref[0])
    def _():
        x = jnp.concatenate([xs_ref[pl.ds(c, tm, stride=ROW_SLABS), :] for c in range(ROW_SLABS)],
                            axis=1).astype(BF16)
        gu = jnp.dot(x, wgu_bf[...], preferred_element_type=F32) + bgu_ref[...]
        gate = jnp.minimum(gu[:, :D_FF], SWIGLU_LIMIT)
        up = jnp.clip(gu[:, D_FF:], -SWIGLU_LIMIT, SWIGLU_LIMIT)
        glu = gate * jax.nn.sigmoid(SWIGLU_ALPHA * gate)
        y = jnp.dot(((up + 1.0) * glu).astype(BF16), wd_bf[...],
                    preferred_element_type=F32) + bd_ref[...]
        for c in range(ROW_SLABS):
            ys_ref[pl.ds(c, tm, stride=ROW_SLABS), :] = y[:, c * LANES:(c + 1) * LANES]


def _experts(tile_expert, n_used, xs, w_gate_up, b_gate_up, w_down, b_down, n_tiles):
    tm = EXP_TILE
    rows = lambda i, te, nu: (jnp.minimum(i, nu[0] - 1), 0)
    wsel = lambda i, te, nu: (te[i], 0, 0)
    grid_spec = pltpu.PrefetchScalarGridSpec(
        num_scalar_prefetch=2,
        grid=(n_tiles,),
        in_specs=[pl.BlockSpec((tm * ROW_SLABS, LANES), rows),
                  pl.BlockSpec((None, D_MODEL, 2 * D_FF), wsel),
                  pl.BlockSpec((None, 1, 2 * D_FF), wsel),
                  pl.BlockSpec((None, D_FF, D_MODEL), wsel),
                  pl.BlockSpec((None, 1, D_MODEL), wsel)],
        out_specs=pl.BlockSpec((tm * ROW_SLABS, LANES), rows),
        scratch_shapes=[pltpu.VMEM((D_MODEL, 2 * D_FF), BF16), pltpu.VMEM((D_FF, D_MODEL), BF16)],
    )
    return pl.pallas_call(
        _expert_body,
        grid_spec=grid_spec,
        out_shape=jax.ShapeDtypeStruct(xs.shape, F32),
        input_output_aliases={2: 0},
        compiler_params=_params(("arbitrary",)),
        name="experts",
    )(tile_expert, n_used, xs, w_gate_up, b_gate_up.reshape(N_EXPERTS, 1, 2 * D_FF),
      w_down, b_down.reshape(N_EXPERTS, 1, D_MODEL))


def _combine_body(dest_ref, x2_ref, mw_ref, g_ref, ys_hbm, o_ref, buf, sem):
    tb = x2_ref.shape[0]

    def issue(j, carry):
        for kk in range(TOP_K):
            d = dest_ref[0, 0, j * TOP_K + kk]
            pltpu.make_async_copy(_row_slab(ys_hbm, d), _row_slab(buf, kk * tb + j), sem).start()
        return carry

    lax.fori_loop(0, tb, issue, 0)
    pltpu.make_async_copy(buf, buf, sem).wait()

    mw = mw_ref[...]
    cols = []
    for c in range(ROW_SLABS):
        acc = x2_ref[:, c * LANES:(c + 1) * LANES]
        for kk in range(TOP_K):
            rows = buf[pl.ds(kk * tb * ROW_SLABS + c, tb, stride=ROW_SLABS), :]
            acc = acc + mw[:, kk:kk + 1] * rows
        cols.append(acc)
    o_ref[...] = _rms(jnp.concatenate(cols, axis=1), g_ref[...])


def _combine(dest, x2, mw, g_final, ys):
    t = x2.shape[0]
    tb = min(ROW_TILE, t)
    nb = t // tb
    dest3 = dest.reshape(nb, 1, tb * TOP_K)
    row = lambda i: (i, 0)
    return pl.pallas_call(
        _combine_body,
        grid=(nb,),
        in_specs=[pl.BlockSpec((1, 1, tb * TOP_K), lambda i: (i, 0, 0), memory_space=pltpu.SMEM),
                  pl.BlockSpec((tb, D_MODEL), row), pl.BlockSpec((tb, LANES), row),
                  _const_spec((1, D_MODEL)), pl.BlockSpec(memory_space=pl.ANY)],
        out_specs=pl.BlockSpec((tb, D_MODEL), row),
        out_shape=jax.ShapeDtypeStruct((t, D_MODEL), F32),
        scratch_shapes=[pltpu.VMEM((TOP_K * tb * ROW_SLABS, LANES), F32), pltpu.SemaphoreType.DMA(())],
        compiler_params=_params(("arbitrary",)),
        name="combine",
    )(dest3, x2, mw, g_final.reshape(1, D_MODEL), ys)


def _block_diag_gates(w_rg, w_ig):
    per = GATE_CHUNK // RNN_BLOCK_DIM
    nchunk = w_rg.shape[0] // per
    eye = jnp.eye(per, dtype=w_rg.dtype)

    def bd(w):
        w = w.reshape(nchunk, per, RNN_BLOCK_DIM, RNN_BLOCK_DIM)
        return jnp.einsum('cpij,pq->cpiqj', w, eye).reshape(nchunk, GATE_CHUNK, GATE_CHUNK)

    return jnp.concatenate([bd(w_rg), bd(w_ig)], axis=-1).astype(BF16)


def kernel(x, g_mix, w_in, conv_w, conv_b, w_rg, b_rg, w_ig, b_ig, lru_lambda, b_f, w_lru_out,
           w_fox_out, w_o, g_ffn, w_router, b_router, w_gate_up, b_gate_up, w_down, b_down, g_final):
    bsz, s, d = x.shape
    assert d == D_MODEL
    t = bsz * s
    x2d = x.reshape(t, d)

    o_q = 2 * D_MODEL
    o_f = 5 * D_MODEL
    o_g = o_f + N_HEADS
    scale = 1.0 / math.sqrt(HEAD_DIM)
    w_wide = jnp.concatenate(
        [w_in[:, :o_q], w_in[:, o_q:o_q + D_MODEL] * scale, w_in[:, o_q + D_MODEL:o_f], w_in[:, o_g:]],
        axis=1).astype(BF16)
    w_f = jnp.pad(w_in[:, o_f:o_g], ((0, 0), (0, LANES - N_HEADS))).astype(BF16)
    b_f_pad = jnp.pad(b_f, (0, LANES - N_HEADS)).reshape(1, LANES)
    w_r = jnp.pad(w_router, ((0, 0), (0, LANES - N_EXPERTS)))
    b_r = jnp.pad(b_router, (0, LANES - N_EXPERTS)).reshape(1, LANES)

    lx, lg, q, k, v, gl, gf, f_logit = _inproj(x2d, g_mix, w_wide, w_f)

    ml = _lru(lx, lg, gl, conv_w, conv_b, _block_diag_gates(w_rg, w_ig), b_rg, b_ig, lru_lambda,
              w_lru_out.astype(BF16), bsz, s)

    cum = _cumsum(f_logit, b_f_pad, bsz, s)
    ct = cum[:, :N_HEADS].reshape(bsz, s, N_HEADS // 2, 2).transpose(0, 2, 3, 1)
    yf = _attention(q, k, v, ct, bsz, s)

    x2, h2s, mi, mw, cnt = _merge(x2d, ml, yf, gf, w_fox_out.astype(BF16), w_o.astype(BF16),
                                  g_ffn, w_r, b_r)

    idx = mi[:, :TOP_K]
    rank = mi[:, TOP_K:2 * TOP_K]
    counts = cnt[0, :N_EXPERTS]
    tiles_e = (counts + EXP_TILE - 1) // EXP_TILE
    tile_end = jnp.cumsum(tiles_e)
    row_off = (tile_end - tiles_e) * EXP_TILE
    dest = (row_off[idx] + rank).astype(I32)
    n_tiles = (t * TOP_K) // EXP_TILE + N_EXPERTS
    n_used = tile_end[-1:].astype(I32)
    tile_ids = jnp.minimum(jnp.arange(n_tiles, dtype=I32), n_used[0] - 1)
    tile_expert = jnp.sum(tile_ids[:, None] >= tile_end[None, :], axis=1).astype(I32)

    xs = _scatter(dest, h2s, n_tiles * EXP_TILE)
    ys = _experts(tile_expert, n_used, xs, w_gate_up, b_gate_up, w_down, b_down, n_tiles)
    out = _combine(dest, x2, mw, g_final, ys)
    return out.reshape(bsz, s, d)
```

```python
import functools
import math

import jax
import jax.numpy as jnp
from jax import lax
from jax.experimental import pallas as pl
from jax.experimental.pallas import tpu as pltpu

F32 = jnp.float32
BF16 = jnp.bfloat16
I32 = jnp.int32

D_MODEL = 1024
RNN_BLOCK_DIM = 64
LRU_C = 8.0
HEAD_DIM = 64
N_HEADS = 16
N_EXPERTS = 32
TOP_K = 4
D_FF = 1024
SWIGLU_LIMIT = 7.0
SWIGLU_ALPHA = 1.702
RMS_EPS = 1e-6
LOG2E = 1.4426950408889634
Q_SCALE = LOG2E / math.sqrt(HEAD_DIM)
Q_GROUP = 2

LANES = 128
SUBLANES = 8
ROW_SLABS = D_MODEL // LANES
VMEM_LIMIT_BYTES = 56 * 1024 * 1024

TOK_TILE = 256
LRU_TILE = 256
ATT_TILE = 512
EXP_TILE = 256
ROW_TILE = 256
CONV_WIDTH = 4
GATE_CHUNK = 256


def _params(semantics):
    return pltpu.CompilerParams(dimension_semantics=semantics, vmem_limit_bytes=VMEM_LIMIT_BYTES)


def _rms(x, g):
    return x * lax.rsqrt(jnp.mean(x * x, axis=-1, keepdims=True) + RMS_EPS) * g


def _const_spec(shape):
    nd = len(shape)
    return pl.BlockSpec(shape, lambda *_: (0,) * nd)


def _inproj_body(x_ref, g_ref, w_ref, wf_ref, *outs):
    h = _rms(x_ref[...], g_ref[...]).astype(BF16)
    *wide, o_f = outs
    for j, o in enumerate(wide):
        acc = jnp.dot(h, w_ref[:, j * D_MODEL:(j + 1) * D_MODEL], preferred_element_type=F32)
        if j == Q_GROUP:
            acc = acc * Q_SCALE
        o[...] = acc.astype(o.dtype)
    o_f[...] = jnp.dot(h, wf_ref[...], preferred_element_type=F32)


def _inproj(x2d, g_mix, w_wide, w_f):
    t = x2d.shape[0]
    n_wide = w_wide.shape[1] // D_MODEL
    tm = min(TOK_TILE, t)
    row = lambda i: (i, 0)
    return pl.pallas_call(
        _inproj_body,
        grid=(t // tm,),
        in_specs=[pl.BlockSpec((tm, D_MODEL), row), _const_spec((1, D_MODEL)),
                  _const_spec(w_wide.shape), _const_spec(w_f.shape)],
        out_specs=[pl.BlockSpec((tm, D_MODEL), row)] * n_wide + [pl.BlockSpec((tm, LANES), row)],
        out_shape=[jax.ShapeDtypeStruct((t, D_MODEL), BF16)] * n_wide
                  + [jax.ShapeDtypeStruct((t, LANES), F32)],
        compiler_params=_params(("arbitrary",)),
        name="inproj",
    )(x2d, g_mix.reshape(1, D_MODEL), w_wide, w_f)


def _lru_body(lx_ref, lg_ref, gl_ref, cw_ref, cb_ref, wbd_ref, brg_ref, big_ref, lam_ref, wo_ref,
              o_ref, xbuf, hcar, a_s, b_s, h_s):
    ts = lx_ref.shape[0]

    @pl.when(pl.program_id(1) == 0)
    def _():
        xbuf[0:SUBLANES, :] = jnp.zeros((SUBLANES, D_MODEL), F32)
        hcar[...] = jnp.zeros_like(hcar)

    x = lx_ref[...].astype(F32)
    xbuf[SUBLANES:SUBLANES + ts, :] = x
    xc = cb_ref[...] + cw_ref[CONV_WIDTH - 1:CONV_WIDTH, :] * x
    for k in range(CONV_WIDTH - 1):
        off = SUBLANES - (CONV_WIDTH - 1) + k
        xc = xc + cw_ref[k:k + 1, :] * xbuf[off:off + ts, :]
    xbuf[0:SUBLANES, :] = xbuf[ts:ts + SUBLANES, :]

    xcb = xc.astype(BF16)
    for c in range(D_MODEL // GATE_CHUNK):
        sl = slice(c * GATE_CHUNK, (c + 1) * GATE_CHUNK)
        g = jnp.dot(xcb[:, sl], wbd_ref[c], preferred_element_type=F32)
        rt = jax.nn.sigmoid(g[:, :GATE_CHUNK] + brg_ref[:, sl])
        it = jax.nn.sigmoid(g[:, GATE_CHUNK:] + big_ref[:, sl])
        nl = -lam_ref[:, sl]
        softplus = jnp.maximum(nl, 0.0) + jnp.log1p(jnp.exp(-jnp.abs(nl)))
        a = jnp.exp(-LRU_C * rt * softplus)
        a_s[:, sl] = a
        b_s[:, sl] = jnp.sqrt(1.0 - a * a) * (it * xc[:, sl])

    row = lax.broadcasted_iota(I32, (SUBLANES, D_MODEL), 0)

    def group(gidx, h0):
        off = pl.multiple_of(gidx * SUBLANES, SUBLANES)
        av = a_s[pl.ds(off, SUBLANES), :]
        bv = b_s[pl.ds(off, SUBLANES), :]
        for d in (1, 2, 4):
            keep = row >= d
            a_sh = jnp.where(keep, pltpu.roll(av, d, 0), 1.0)
            b_sh = jnp.where(keep, pltpu.roll(bv, d, 0), 0.0)
            bv = av * b_sh + bv
            av = av * a_sh
        hv = av * h0 + bv
        h_s[pl.ds(off, SUBLANES), :] = hv
        return jnp.broadcast_to(hv[SUBLANES - 1:SUBLANES, :], (SUBLANES, D_MODEL))

    hcar[...] = lax.fori_loop(0, ts // SUBLANES, group, hcar[...])

    y = (h_s[...] * jax.nn.gelu(lg_ref[...].astype(F32), approximate=True)).astype(BF16)
    proj = jnp.dot(y, wo_ref[...], preferred_element_type=F32)
    o_ref[...] = (jax.nn.sigmoid(gl_ref[...].astype(F32)) * proj).astype(o_ref.dtype)


def _lru(lx, lg, gl, conv_w, conv_b, wbd, b_rg, b_ig, lam, w_out, bsz, s):
    ts = min(LRU_TILE, s)
    ns = s // ts
    tile = pl.BlockSpec((ts, D_MODEL), lambda b, i: (b * ns + i, 0))
    vec = lambda a: a.reshape(1, D_MODEL)
    return pl.pallas_call(
        _lru_body,
        grid=(bsz, ns),
        in_specs=[tile, tile, tile, _const_spec((CONV_WIDTH, D_MODEL)), _const_spec((1, D_MODEL)),
                  _const_spec(wbd.shape), _const_spec((1, D_MODEL)), _const_spec((1, D_MODEL)),
                  _const_spec((1, D_MODEL)), _const_spec(w_out.shape)],
        out_specs=tile,
        out_shape=jax.ShapeDtypeStruct((bsz * s, D_MODEL), BF16),
        scratch_shapes=[pltpu.VMEM((ts + SUBLANES, D_MODEL), F32), pltpu.VMEM((SUBLANES, D_MODEL), F32),
                        pltpu.VMEM((ts, D_MODEL), F32), pltpu.VMEM((ts, D_MODEL), F32),
                        pltpu.VMEM((ts, D_MODEL), F32)],
        compiler_params=_params(("arbitrary", "arbitrary")),
        name="lru",
    )(lx, lg, gl, conv_w, vec(conv_b), wbd, vec(b_rg), vec(b_ig), vec(lam), w_out)


def _cumsum_body(f_ref, bf_ref, o_ref, carry):
    tc = f_ref.shape[0]

    @pl.when(pl.program_id(1) == 0)
    def _():
        carry[...] = jnp.zeros_like(carry)

    z = f_ref[...] + bf_ref[...]
    logf = jnp.minimum(z, 0.0) - jnp.log1p(jnp.exp(-jnp.abs(z)))
    r = lax.broadcasted_iota(I32, (tc, tc), 0)
    c = lax.broadcasted_iota(I32, (tc, tc), 1)
    tri = (r >= c).astype(F32)
    cum = jnp.dot(tri, logf, preferred_element_type=F32, precision=lax.Precision.HIGHEST)
    cum = cum + carry[0:1, :]
    o_ref[...] = cum
    carry[...] = jnp.broadcast_to(cum[tc - 1:tc, :], carry.shape)


def _cumsum(f_logit, b_f_pad, bsz, s):
    tc = min(TOK_TILE, s)
    ns = s // tc
    tile = pl.BlockSpec((tc, LANES), lambda b, i: (b * ns + i, 0))
    return pl.pallas_call(
        _cumsum_body,
        grid=(bsz, ns),
        in_specs=[tile, _const_spec((1, LANES))],
        out_specs=tile,
        out_shape=jax.ShapeDtypeStruct((bsz * s, LANES), F32),
        scratch_shapes=[pltpu.VMEM((SUBLANES, LANES), F32)],
        compiler_params=_params(("arbitrary", "arbitrary")),
        name="cumsum",
    )(f_logit, b_f_pad)


def _col_form(row_ref, h, start, n):
    blocks = []
    for j in range(n // LANES):
        r = row_ref[h:h + 1, pl.ds(start + j * LANES, LANES)]
        blocks.append(jnp.broadcast_to(r, (LANES, LANES)).T)
    return jnp.concatenate(blocks, axis=0)


def _split3(c):
    hi = c.astype(BF16).astype(F32)
    rest = c - hi
    mid = rest.astype(BF16).astype(F32)
    return hi, mid, rest - mid


def _attn_body(q_ref, k_ref, v_ref, cq_ref, ck_ref, o_ref, ka_s, va_s, qa_s, m_s, acc_s):
    tq = q_ref.shape[0]
    tk = tq
    s_len = k_ref.shape[0]
    qi = pl.program_id(2)
    lane = lax.broadcasted_iota(I32, (1, LANES), 1)
    is_head = lane < HEAD_DIM

    @pl.when(qi == 0)
    def _():
        def chunk(ci, carry):
            off = pl.multiple_of(ci * tk, tk)
            k2 = k_ref[pl.ds(off, tk), :].astype(F32)
            v2 = v_ref[pl.ds(off, tk), :].astype(F32)
            for h in range(2):
                kh = k2 if h == 0 else pltpu.roll(k2, HEAD_DIM, 1)
                vh = v2 if h == 0 else pltpu.roll(v2, HEAD_DIM, 1)
                hi, mid, lo = _split3(_col_form(ck_ref, h, off, tk) * LOG2E)
                ext = jnp.where(lane < HEAD_DIM + 3, 1.0,
                                jnp.where(lane == HEAD_DIM + 3, -hi,
                                          jnp.where(lane == HEAD_DIM + 4, -mid,
                                                    jnp.where(lane == HEAD_DIM + 5, -lo, 0.0))))
                ka_s[h, pl.ds(off, tk), :] = jnp.where(is_head, kh, ext).astype(BF16)
                ones_col = jnp.where(lane == HEAD_DIM, 1.0, 0.0)
                va_s[h, pl.ds(off, tk), :] = jnp.where(is_head, vh, ones_col).astype(BF16)
            return carry

        lax.fori_loop(0, s_len // tk, chunk, 0)

    q2 = q_ref[...].astype(F32)
    for h in range(2):
        qh = q2 if h == 0 else pltpu.roll(q2, HEAD_DIM, 1)
        hi, mid, lo = _split3(_col_form(cq_ref, h, 0, tq) * LOG2E)
        ext = jnp.where(lane == HEAD_DIM, hi,
                        jnp.where(lane == HEAD_DIM + 1, mid,
                                  jnp.where(lane == HEAD_DIM + 2, lo,
                                            jnp.where(lane < HEAD_DIM + 6, 1.0, 0.0))))
        qa_s[h] = jnp.where(is_head, qh, ext).astype(BF16)
    m_s[...] = jnp.full_like(m_s, -jnp.inf)
    acc_s[...] = jnp.zeros_like(acc_s)

    def step(ki, masked):
        off = pl.multiple_of(ki * tk, tk)
        for h in range(2):
            sc = lax.dot_general(qa_s[h], ka_s[h, pl.ds(off, tk), :], (((1,), (1,)), ((), ())),
                                 preferred_element_type=F32)
            if masked:
                r = lax.broadcasted_iota(I32, (tq, tk), 0)
                c = lax.broadcasted_iota(I32, (tq, tk), 1)
                sc = jnp.where(c <= r, sc, -jnp.inf)
            m_prev = m_s[h]
            m_new = jnp.maximum(m_prev, jnp.max(sc, axis=1, keepdims=True))
            p = jnp.concatenate([jnp.exp2(sc[:, j * LANES:(j + 1) * LANES] - m_new)
                                 for j in range(tk // LANES)], axis=1)
            m_s[h] = m_new
            acc_s[h] = jnp.exp2(m_prev - m_new) * acc_s[h] + jnp.dot(
                p.astype(BF16), va_s[h, pl.ds(off, tk), :], preferred_element_type=F32)

    def full_step(ki, carry):
        step(ki, False)
        return carry

    lax.fori_loop(0, qi, full_step, 0)
    step(qi, True)
    o0 = acc_s[0] / acc_s[0][:, HEAD_DIM:HEAD_DIM + 1]
    o1 = acc_s[1] / acc_s[1][:, HEAD_DIM:HEAD_DIM + 1]
    o_ref[...] = jnp.where(is_head, o0, pltpu.roll(o1, HEAD_DIM, 1)).astype(o_ref.dtype)


def _attention(q, k, v, ct, bsz, s):
    tq = min(ATT_TILE, s)
    nq = s // tq
    npair = N_HEADS // 2
    qspec = pl.BlockSpec((tq, LANES), lambda b, p, i: (b * nq + i, p))
    kvspec = pl.BlockSpec((s, LANES), lambda b, p, i: (b, p))
    return pl.pallas_call(
        _attn_body,
        grid=(bsz, npair, nq),
        in_specs=[qspec, kvspec, kvspec,
                  pl.BlockSpec((None, None, 2, tq), lambda b, p, i: (b, p, 0, i)),
                  pl.BlockSpec((None, None, 2, s), lambda b, p, i: (b, p, 0, 0))],
        out_specs=qspec,
        out_shape=jax.ShapeDtypeStruct((bsz * s, D_MODEL), BF16),
        scratch_shapes=[pltpu.VMEM((2, s, LANES), BF16), pltpu.VMEM((2, s, LANES), BF16),
                        pltpu.VMEM((2, tq, LANES), BF16), pltpu.VMEM((2, tq, LANES), F32),
                        pltpu.VMEM((2, tq, LANES), F32)],
        compiler_params=_params(("arbitrary", "arbitrary", "arbitrary")),
        name="attn",
    )(q, k, v, ct, ct)


def _merge_body(x_ref, ml_ref, yf_ref, gf_ref, wfo_ref, wo_ref, g_ref, wr_ref, br_ref,
                x2_ref, h2_ref, mi_ref, mw_ref, cnt_ref, running):
    tm = x_ref.shape[0]

    @pl.when(pl.program_id(0) == 0)
    def _():
        running[...] = jnp.zeros_like(running)

    fox = jnp.dot(yf_ref[...], wfo_ref[...], preferred_element_type=F32)
    merged = ml_ref[...].astype(F32) + jax.nn.sigmoid(gf_ref[...].astype(F32)) * fox
    x2 = x_ref[...] + jnp.dot(merged.astype(BF16), wo_ref[...], preferred_element_type=F32)
    x2_ref[...] = x2
    h2 = _rms(x2, g_ref[...])
    for c in range(ROW_SLABS):
        h2_ref[pl.ds(c, tm, stride=ROW_SLABS), :] = h2[:, c * LANES:(c + 1) * LANES]

    logits = jnp.dot(h2, wr_ref[...], preferred_element_type=F32,
                     precision=lax.Precision.HIGHEST) + br_ref[...]
    lane = lax.broadcasted_iota(I32, (tm, LANES), 1)
    logits = jnp.where(lane < N_EXPERTS, logits, -jnp.inf)
    vals, idxs = [], []
    for _ in range(TOP_K):
        mx = jnp.max(logits, axis=1, keepdims=True)
        ix = jnp.min(jnp.where(logits == mx, lane, LANES), axis=1, keepdims=True)
        vals.append(mx)
        idxs.append(ix)
        logits = jnp.where(lane == ix, -jnp.inf, logits)
    exps = [jnp.exp(vk - vals[0]) for vk in vals]
    denom = exps[0] + exps[1] + exps[2] + exps[3]

    onehot = jnp.zeros((tm, LANES), F32)
    for ix in idxs:
        onehot = onehot + (lane == ix).astype(F32)
    r = lax.broadcasted_iota(I32, (tm, tm), 0)
    c = lax.broadcasted_iota(I32, (tm, tm), 1)
    before = jnp.dot((c < r).astype(BF16), onehot.astype(BF16), preferred_element_type=F32)
    pos = before + running[0:1, :]
    mi = jnp.zeros((tm, LANES), I32)
    mw = jnp.zeros((tm, LANES), F32)
    for kk in range(TOP_K):
        rank = jnp.sum(jnp.where(lane == idxs[kk], pos, 0.0), axis=1, keepdims=True).astype(I32)
        mi = jnp.where(lane == kk, idxs[kk], mi)
        mi = jnp.where(lane == TOP_K + kk, rank, mi)
        mw = jnp.where(lane == kk, exps[kk] / denom, mw)
    mi_ref[...] = mi
    mw_ref[...] = mw
    total = running[0:1, :] + jnp.sum(onehot, axis=0, keepdims=True)
    running[...] = jnp.broadcast_to(total, running.shape)
    cnt_ref[...] = jnp.broadcast_to(total, cnt_ref.shape).astype(I32)


def _merge(x2d, ml, yf, gf, w_fo, w_o, g_ffn, w_r, b_r):
    t = x2d.shape[0]
    tm = min(TOK_TILE, t)
    row = lambda i: (i, 0)
    tile = pl.BlockSpec((tm, D_MODEL), row)
    meta = pl.BlockSpec((tm, LANES), row)
    return pl.pallas_call(
        _merge_body,
        grid=(t // tm,),
        in_specs=[tile, tile, tile, tile, _const_spec(w_fo.shape), _const_spec(w_o.shape),
                  _const_spec((1, D_MODEL)), _const_spec(w_r.shape), _const_spec((1, LANES))],
        out_specs=[tile, pl.BlockSpec((tm * ROW_SLABS, LANES), row), meta, meta,
                   _const_spec((SUBLANES, LANES))],
        out_shape=[jax.ShapeDtypeStruct((t, D_MODEL), F32),
                   jax.ShapeDtypeStruct((t * ROW_SLABS, LANES), F32),
                   jax.ShapeDtypeStruct((t, LANES), I32),
                   jax.ShapeDtypeStruct((t, LANES), F32),
                   jax.ShapeDtypeStruct((SUBLANES, LANES), I32)],
        scratch_shapes=[pltpu.VMEM((SUBLANES, LANES), F32)],
        compiler_params=_params(("arbitrary",)),
        name="merge",
    )(x2d, ml, yf, gf, w_fo, w_o, g_ffn.reshape(1, D_MODEL), w_r, b_r)


def _row_slab(ref, row):
    return ref.at[pl.ds(pl.multiple_of(row * ROW_SLABS, ROW_SLABS), ROW_SLABS), :]


def _scatter_body(dest_ref, h2_ref, xs_in_hbm, xs_hbm, sem):
    del xs_in_hbm
    tb = dest_ref.shape[-1] // TOP_K

    def issue(j, carry):
        for kk in range(TOP_K):
            d = dest_ref[0, 0, j * TOP_K + kk]
            pltpu.make_async_copy(_row_slab(h2_ref, j), _row_slab(xs_hbm, d), sem).start()
        return carry

    lax.fori_loop(0, tb, issue, 0)
    n = tb * TOP_K * ROW_SLABS
    pltpu.make_async_copy(xs_hbm.at[pl.ds(0, n), :], xs_hbm.at[pl.ds(0, n), :], sem).wait()


def _scatter(dest, h2s, n_rows):
    t = dest.shape[0]
    tb = min(ROW_TILE, t)
    nb = t // tb
    dest3 = dest.reshape(nb, 1, tb * TOP_K)
    zeros = jnp.zeros((n_rows * ROW_SLABS, LANES), F32)
    return pl.pallas_call(
        _scatter_body,
        grid=(nb,),
        in_specs=[pl.BlockSpec((1, 1, tb * TOP_K), lambda i: (i, 0, 0), memory_space=pltpu.SMEM),
                  pl.BlockSpec((tb * ROW_SLABS, LANES), lambda i: (i, 0)),
                  pl.BlockSpec(memory_space=pl.ANY)],
        out_specs=pl.BlockSpec(memory_space=pl.ANY),
        out_shape=jax.ShapeDtypeStruct((n_rows * ROW_SLABS, LANES), F32),
        scratch_shapes=[pltpu.SemaphoreType.DMA(())],
        input_output_aliases={2: 0},
        compiler_params=_params(("arbitrary",)),
        name="scatter",
    )(dest3, h2s, zeros)


def _expert_body(te_ref, nu_ref, xs_ref, wgu_ref, bgu_ref, wd_ref, bd_ref, ys_ref, wgu_bf, wd_bf):
    i = pl.program_id(0)
    tm = xs_ref.shape[0] // ROW_SLABS
    e = te_ref[i]
    e_prev = te_ref[jnp.maximum(i - 1, 0)]

    @pl.when(jnp.logical_or(i == 0, e != e_prev))
    def _():
        wgu_bf[...] = wgu_ref[...].astype(BF16)
        wd_bf[...] = wd_ref[...].astype(BF16)

    @pl.when(i < nu_ref[0])
    def _():
        x = jnp.concatenate([xs_ref[pl.ds(c, tm, stride=ROW_SLABS), :] for c in range(ROW_SLABS)],
                            axis=1).astype(BF16)
        gu = jnp.dot(x, wgu_bf[...], preferred_element_type=F32) + bgu_ref[...]
        gate = jnp.minimum(gu[:, :D_FF], SWIGLU_LIMIT)
        up = jnp.clip(gu[:, D_FF:], -SWIGLU_LIMIT, SWIGLU_LIMIT)
        glu = gate * jax.nn.sigmoid(SWIGLU_ALPHA * gate)
        y = jnp.dot(((up + 1.0) * glu).astype(BF16), wd_bf[...],
                    preferred_element_type=F32) + bd_ref[...]
        for c in range(ROW_SLABS):
            ys_ref[pl.ds(c, tm, stride=ROW_SLABS), :] = y[:, c * LANES:(c + 1) * LANES]


def _experts(tile_expert, n_used, xs, w_gate_up, b_gate_up, w_down, b_down, n_tiles):
    tm = EXP_TILE
    rows = lambda i, te, nu: (jnp.minimum(i, nu[0] - 1), 0)
    wsel = lambda i, te, nu: (te[i], 0, 0)
    grid_spec = pltpu.PrefetchScalarGridSpec(
        num_scalar_prefetch=2,
        grid=(n_tiles,),
        in_specs=[pl.BlockSpec((tm * ROW_SLABS, LANES), rows),
                  pl.BlockSpec((None, D_MODEL, 2 * D_FF), wsel),
                  pl.BlockSpec((None, 1, 2 * D_FF), wsel),
                  pl.BlockSpec((None, D_FF, D_MODEL), wsel),
                  pl.BlockSpec((None, 1, D_MODEL), wsel)],
        out_specs=pl.BlockSpec((tm * ROW_SLABS, LANES), rows),
        scratch_shapes=[pltpu.VMEM((D_MODEL, 2 * D_FF), BF16), pltpu.VMEM((D_FF, D_MODEL), BF16)],
    )
    return pl.pallas_call(
        _expert_body,
        grid_spec=grid_spec,
        out_shape=jax.ShapeDtypeStruct(xs.shape, F32),
        input_output_aliases={2: 0},
        compiler_params=_params(("arbitrary",)),
        name="experts",
    )(tile_expert, n_used, xs, w_gate_up, b_gate_up.reshape(N_EXPERTS, 1, 2 * D_FF),
      w_down, b_down.reshape(N_EXPERTS, 1, D_MODEL))


def _combine_body(dest_ref, x2_ref, mw_ref, g_ref, ys_hbm, o_ref, buf, sem):
    tb = x2_ref.shape[0]

    def issue(j, carry):
        for kk in range(TOP_K):
            d = dest_ref[0, 0, j * TOP_K + kk]
            pltpu.make_async_copy(_row_slab(ys_hbm, d), _row_slab(buf, kk * tb + j), sem).start()
        return carry

    lax.fori_loop(0, tb, issue, 0)
    pltpu.make_async_copy(buf, buf, sem).wait()

    mw = mw_ref[...]
    cols = []
    for c in range(ROW_SLABS):
        acc = x2_ref[:, c * LANES:(c + 1) * LANES]
        for kk in range(TOP_K):
            rows = buf[pl.ds(kk * tb * ROW_SLABS + c, tb, stride=ROW_SLABS), :]
            acc = acc + mw[:, kk:kk + 1] * rows
        cols.append(acc)
    o_ref[...] = _rms(jnp.concatenate(cols, axis=1), g_ref[...])


def _combine(dest, x2, mw, g_final, ys):
    t = x2.shape[0]
    tb = min(ROW_TILE, t)
    nb = t // tb
    dest3 = dest.reshape(nb, 1, tb * TOP_K)
    row = lambda i: (i, 0)
    return pl.pallas_call(
        _combine_body,
        grid=(nb,),
        in_specs=[pl.BlockSpec((1, 1, tb * TOP_K), lambda i: (i, 0, 0), memory_space=pltpu.SMEM),
                  pl.BlockSpec((tb, D_MODEL), row), pl.BlockSpec((tb, LANES), row),
                  _const_spec((1, D_MODEL)), pl.BlockSpec(memory_space=pl.ANY)],
        out_specs=pl.BlockSpec((tb, D_MODEL), row),
        out_shape=jax.ShapeDtypeStruct((t, D_MODEL), F32),
        scratch_shapes=[pltpu.VMEM((TOP_K * tb * ROW_SLABS, LANES), F32), pltpu.SemaphoreType.DMA(())],
        compiler_params=_params(("arbitrary",)),
        name="combine",
    )(dest3, x2, mw, g_final.reshape(1, D_MODEL), ys)


def _block_diag_gates(w_rg, w_ig):
    per = GATE_CHUNK // RNN_BLOCK_DIM
    nchunk = w_rg.shape[0] // per
    eye = jnp.eye(per, dtype=w_rg.dtype)

    def bd(w):
        w = w.reshape(nchunk, per, RNN_BLOCK_DIM, RNN_BLOCK_DIM)
        return jnp.einsum('cpij,pq->cpiqj', w, eye).reshape(nchunk, GATE_CHUNK, GATE_CHUNK)

    return jnp.concatenate([bd(w_rg), bd(w_ig)], axis=-1).astype(BF16)


def kernel(x, g_mix, w_in, conv_w, conv_b, w_rg, b_rg, w_ig, b_ig, lru_lambda, b_f, w_lru_out,
           w_fox_out, w_o, g_ffn, w_router, b_router, w_gate_up, b_gate_up, w_down, b_down, g_final):
    bsz, s, d = x.shape
    assert d == D_MODEL
    t = bsz * s
    x2d = x.reshape(t, d)

    o_f = 5 * D_MODEL
    o_g = o_f + N_HEADS
    w_wide = jnp.concatenate([w_in[:, :o_f], w_in[:, o_g:]], axis=1).astype(BF16)
    w_f = jnp.pad(w_in[:, o_f:o_g], ((0, 0), (0, LANES - N_HEADS))).astype(BF16)
    b_f_pad = jnp.pad(b_f, (0, LANES - N_HEADS)).reshape(1, LANES)
    w_r = jnp.pad(w_router, ((0, 0), (0, LANES - N_EXPERTS)))
    b_r = jnp.pad(b_router, (0, LANES - N_EXPERTS)).reshape(1, LANES)

    lx, lg, q, k, v, gl, gf, f_logit = _inproj(x2d, g_mix, w_wide, w_f)

    ml = _lru(lx, lg, gl, conv_w, conv_b, _block_diag_gates(w_rg, w_ig), b_rg, b_ig, lru_lambda,
              w_lru_out.astype(BF16), bsz, s)

    cum = _cumsum(f_logit, b_f_pad, bsz, s)
    ct = cum[:, :N_HEADS].reshape(bsz, s, N_HEADS // 2, 2).transpose(0, 2, 3, 1)
    yf = _attention(q, k, v, ct, bsz, s)

    x2, h2s, mi, mw, cnt = _merge(x2d, ml, yf, gf, w_fox_out.astype(BF16), w_o.astype(BF16),
                                  g_ffn, w_r, b_r)

    idx = mi[:, :TOP_K]
    rank = mi[:, TOP_K:2 * TOP_K]
    counts = cnt[0, :N_EXPERTS]
    tiles_e = (counts + EXP_TILE - 1) // EXP_TILE
    tile_end = jnp.cumsum(tiles_e)
    row_off = (tile_end - tiles_e) * EXP_TILE
    dest = (row_off[idx] + rank).astype(I32)
    n_tiles = (t * TOP_K) // EXP_TILE + N_EXPERTS
    n_used = tile_end[-1:].astype(I32)
    tile_ids = jnp.minimum(jnp.arange(n_tiles, dtype=I32), n_used[0] - 1)
    tile_expert = jnp.sum(tile_ids[:, None] >= tile_end[None, :], axis=1).astype(I32)

    xs = _scatter(dest, h2s, n_tiles * EXP_TILE)
    ys = _experts(tile_expert, n_used, xs, w_gate_up, b_gate_up, w_down, b_down, n_tiles)
    out = _combine(dest, x2, mw, g_final, ys)
    return out.reshape(bsz, s, d)
```

```python
import functools
import math

import jax
import jax.numpy as jnp
from jax import lax
from jax.experimental import pallas as pl
from jax.experimental.pallas import tpu as pltpu

F32 = jnp.float32
BF16 = jnp.bfloat16
I32 = jnp.int32

D_MODEL = 1024
RNN_BLOCK_DIM = 64
LRU_C = 8.0
HEAD_DIM = 64
N_HEADS = 16
N_EXPERTS = 32
TOP_K = 4
D_FF = 1024
SWIGLU_LIMIT = 7.0
SWIGLU_ALPHA = 1.702
RMS_EPS = 1e-6
LOG2E = 1.4426950408889634
Q_SCALE = LOG2E / math.sqrt(HEAD_DIM)
Q_GROUP = 2

LANES = 128
SUBLANES = 8
ROW_SLABS = D_MODEL // LANES
VMEM_LIMIT_BYTES = 56 * 1024 * 1024

TOK_TILE = 256
LRU_TILE = 256
ATT_TILE = 512
EXP_TILE = 256
ROW_TILE = 256
CONV_WIDTH = 4
GATE_CHUNK = 256


def _params(semantics):
    return pltpu.CompilerParams(dimension_semantics=semantics, vmem_limit_bytes=VMEM_LIMIT_BYTES)


def _rms(x, g):
    return x * lax.rsqrt(jnp.mean(x * x, axis=-1, keepdims=True) + RMS_EPS) * g


def _const_spec(shape):
    nd = len(shape)
    return pl.BlockSpec(shape, lambda *_: (0,) * nd)


def _inproj_body(x_ref, g_ref, w_ref, wf_ref, *outs):
    h = _rms(x_ref[...], g_ref[...]).astype(BF16)
    *wide, o_f = outs
    for j, o in enumerate(wide):
        acc = jnp.dot(h, w_ref[:, j * D_MODEL:(j + 1) * D_MODEL], preferred_element_type=F32)
        if j == Q_GROUP:
            acc = acc * Q_SCALE
        o[...] = acc.astype(o.dtype)
    o_f[...] = jnp.dot(h, wf_ref[...], preferred_element_type=F32)


def _inproj(x2d, g_mix, w_wide, w_f):
    t = x2d.shape[0]
    n_wide = w_wide.shape[1] // D_MODEL
    tm = min(TOK_TILE, t)
    row = lambda i: (i, 0)
    return pl.pallas_call(
        _inproj_body,
        grid=(t // tm,),
        in_specs=[pl.BlockSpec((tm, D_MODEL), row), _const_spec((1, D_MODEL)),
                  _const_spec(w_wide.shape), _const_spec(w_f.shape)],
        out_specs=[pl.BlockSpec((tm, D_MODEL), row)] * n_wide + [pl.BlockSpec((tm, LANES), row)],
        out_shape=[jax.ShapeDtypeStruct((t, D_MODEL), BF16)] * n_wide
                  + [jax.ShapeDtypeStruct((t, LANES), F32)],
        compiler_params=_params(("arbitrary",)),
        name="inproj",
    )(x2d, g_mix.reshape(1, D_MODEL), w_wide, w_f)


def _lru_body(lx_ref, lg_ref, gl_ref, cw_ref, cb_ref, wbd_ref, brg_ref, big_ref, lam_ref, wo_ref,
              o_ref, xbuf, hcar, a_s, b_s, h_s):
    ts = lx_ref.shape[0]

    @pl.when(pl.program_id(1) == 0)
    def _():
        xbuf[0:SUBLANES, :] = jnp.zeros((SUBLANES, D_MODEL), F32)
        hcar[...] = jnp.zeros_like(hcar)

    x = lx_ref[...].astype(F32)
    xbuf[SUBLANES:SUBLANES + ts, :] = x
    xc = cb_ref[...] + cw_ref[CONV_WIDTH - 1:CONV_WIDTH, :] * x
    for k in range(CONV_WIDTH - 1):
        off = SUBLANES - (CONV_WIDTH - 1) + k
        xc = xc + cw_ref[k:k + 1, :] * xbuf[off:off + ts, :]
    xbuf[0:SUBLANES, :] = xbuf[ts:ts + SUBLANES, :]

    xcb = xc.astype(BF16)
    for c in range(D_MODEL // GATE_CHUNK):
        sl = slice(c * GATE_CHUNK, (c + 1) * GATE_CHUNK)
        g = jnp.dot(xcb[:, sl], wbd_ref[c], preferred_element_type=F32)
        rt = jax.nn.sigmoid(g[:, :GATE_CHUNK] + brg_ref[:, sl])
        it = jax.nn.sigmoid(g[:, GATE_CHUNK:] + big_ref[:, sl])
        nl = -lam_ref[:, sl]
        softplus = jnp.maximum(nl, 0.0) + jnp.log1p(jnp.exp(-jnp.abs(nl)))
        a = jnp.exp(-LRU_C * rt * softplus)
        a_s[:, sl] = a
        b_s[:, sl] = jnp.sqrt(1.0 - a * a) * (it * xc[:, sl])

    row = lax.broadcasted_iota(I32, (SUBLANES, D_MODEL), 0)

    def group(gidx, h0):
        off = pl.multiple_of(gidx * SUBLANES, SUBLANES)
        av = a_s[pl.ds(off, SUBLANES), :]
        bv = b_s[pl.ds(off, SUBLANES), :]
        for d in (1, 2, 4):
            keep = row >= d
            a_sh = jnp.where(keep, pltpu.roll(av, d, 0), 1.0)
            b_sh = jnp.where(keep, pltpu.roll(bv, d, 0), 0.0)
            bv = av * b_sh + bv
            av = av * a_sh
        hv = av * h0 + bv
        h_s[pl.ds(off, SUBLANES), :] = hv
        return jnp.broadcast_to(hv[SUBLANES - 1:SUBLANES, :], (SUBLANES, D_MODEL))

    hcar[...] = lax.fori_loop(0, ts // SUBLANES, group, hcar[...])

    y = (h_s[...] * jax.nn.gelu(lg_ref[...].astype(F32), approximate=True)).astype(BF16)
    proj = jnp.dot(y, wo_ref[...], preferred_element_type=F32)
    o_ref[...] = (jax.nn.sigmoid(gl_ref[...].astype(F32)) * proj).astype(o_ref.dtype)


def _lru(lx, lg, gl, conv_w, conv_b, wbd, b_rg, b_ig, lam, w_out, bsz, s):
    ts = min(LRU_TILE, s)
    ns = s // ts
    tile = pl.BlockSpec((ts, D_MODEL), lambda b, i: (b * ns + i, 0))
    vec = lambda a: a.reshape(1, D_MODEL)
    return pl.pallas_call(
        _lru_body,
        grid=(bsz, ns),
        in_specs=[tile, tile, tile, _const_spec((CONV_WIDTH, D_MODEL)), _const_spec((1, D_MODEL)),
                  _const_spec(wbd.shape), _const_spec((1, D_MODEL)), _const_spec((1, D_MODEL)),
                  _const_spec((1, D_MODEL)), _const_spec(w_out.shape)],
        out_specs=tile,
        out_shape=jax.ShapeDtypeStruct((bsz * s, D_MODEL), BF16),
        scratch_shapes=[pltpu.VMEM((ts + SUBLANES, D_MODEL), F32), pltpu.VMEM((SUBLANES, D_MODEL), F32),
                        pltpu.VMEM((ts, D_MODEL), F32), pltpu.VMEM((ts, D_MODEL), F32),
                        pltpu.VMEM((ts, D_MODEL), F32)],
        compiler_params=_params(("arbitrary", "arbitrary")),
        name="lru",
    )(lx, lg, gl, conv_w, vec(conv_b), wbd, vec(b_rg), vec(b_ig), vec(lam), w_out)


def _cumsum_body(f_ref, bf_ref, o_ref, carry):
    tc = f_ref.shape[0]

    @pl.when(pl.program_id(1) == 0)
    def _():
        carry[...] = jnp.zeros_like(carry)

    z = f_ref[...] + bf_ref[...]
    logf = jnp.minimum(z, 0.0) - jnp.log1p(jnp.exp(-jnp.abs(z)))
    r = lax.broadcasted_iota(I32, (tc, tc), 0)
    c = lax.broadcasted_iota(I32, (tc, tc), 1)
    tri = (r >= c).astype(F32)
    cum = jnp.dot(tri, logf, preferred_element_type=F32, precision=lax.Precision.HIGHEST)
    cum = cum + carry[0:1, :]
    o_ref[...] = cum
    carry[...] = jnp.broadcast_to(cum[tc - 1:tc, :], carry.shape)


def _cumsum(f_logit, b_f_pad, bsz, s):
    tc = min(TOK_TILE, s)
    ns = s // tc
    tile = pl.BlockSpec((tc, LANES), lambda b, i: (b * ns + i, 0))
    return pl.pallas_call(
        _cumsum_body,
        grid=(bsz, ns),
        in_specs=[tile, _const_spec((1, LANES))],
        out_specs=tile,
        out_shape=jax.ShapeDtypeStruct((bsz * s, LANES), F32),
        scratch_shapes=[pltpu.VMEM((SUBLANES, LANES), F32)],
        compiler_params=_params(("arbitrary", "arbitrary")),
        name="cumsum",
    )(f_logit, b_f_pad)


def _col_form(row_ref, h, start, n):
    blocks = []
    for j in range(n // LANES):
        r = row_ref[h:h + 1, pl.ds(start + j * LANES, LANES)]
        blocks.append(jnp.broadcast_to(r, (LANES, LANES)).T)
    return jnp.concatenate(blocks, axis=0)


def _split3(c):
    hi = c.astype(BF16).astype(F32)
    rest = c - hi
    mid = rest.astype(BF16).astype(F32)
    return hi, mid, rest - mid


def _attn_body(q_ref, k_ref, v_ref, cq_ref, ck_ref, o_ref, ka_s, va_s, qa_s, m_s, acc_s):
    tq = q_ref.shape[0]
    tk = tq
    s_len = k_ref.shape[0]
    qi = pl.program_id(2)
    lane = lax.broadcasted_iota(I32, (1, LANES), 1)
    is_head = lane < HEAD_DIM

    @pl.when(qi == 0)
    def _():
        def chunk(ci, carry):
            off = pl.multiple_of(ci * tk, tk)
            k2 = k_ref[pl.ds(off, tk), :].astype(F32)
            v2 = v_ref[pl.ds(off, tk), :].astype(F32)
            for h in range(2):
                kh = k2 if h == 0 else pltpu.roll(k2, HEAD_DIM, 1)
                vh = v2 if h == 0 else pltpu.roll(v2, HEAD_DIM, 1)
                hi, mid, lo = _split3(_col_form(ck_ref, h, off, tk) * LOG2E)
                ext = jnp.where(lane < HEAD_DIM + 3, 1.0,
                                jnp.where(lane == HEAD_DIM + 3, -hi,
                                          jnp.where(lane == HEAD_DIM + 4, -mid,
                                                    jnp.where(lane == HEAD_DIM + 5, -lo, 0.0))))
                ka_s[h, pl.ds(off, tk), :] = jnp.where(is_head, kh, ext).astype(BF16)
                ones_col = jnp.where(lane == HEAD_DIM, 1.0, 0.0)
                va_s[h, pl.ds(off, tk), :] = jnp.where(is_head, vh, ones_col).astype(BF16)
            return carry

        lax.fori_loop(0, s_len // tk, chunk, 0)

    q2 = q_ref[...].astype(F32)
    for h in range(2):
        qh = q2 if h == 0 else pltpu.roll(q2, HEAD_DIM, 1)
        hi, mid, lo = _split3(_col_form(cq_ref, h, 0, tq) * LOG2E)
        ext = jnp.where(lane == HEAD_DIM, hi,
                        jnp.where(lane == HEAD_DIM + 1, mid,
                                  jnp.where(lane == HEAD_DIM + 2, lo,
                                            jnp.where(lane < HEAD_DIM + 6, 1.0, 0.0))))
        qa_s[h] = jnp.where(is_head, qh, ext).astype(BF16)
    m_s[...] = jnp.full_like(m_s, -jnp.inf)
    acc_s[...] = jnp.zeros_like(acc_s)

    def step(ki, masked):
        off = pl.multiple_of(ki * tk, tk)
        for h in range(2):
            sc = lax.dot_general(qa_s[h], ka_s[h, pl.ds(off, tk), :], (((1,), (1,)), ((), ())),
                                 preferred_element_type=F32)
            if masked:
                r = lax.broadcasted_iota(I32, (tq, tk), 0)
                c = lax.broadcasted_iota(I32, (tq, tk), 1)
                sc = jnp.where(c <= r, sc, -jnp.inf)
            m_prev = m_s[h]
            m_new = jnp.maximum(m_prev, jnp.max(sc, axis=1, keepdims=True))
            p = jnp.concatenate([jnp.exp2(sc[:, j * LANES:(j + 1) * LANES] - m_new)
                                 for j in range(tk // LANES)], axis=1)
            m_s[h] = m_new
            acc_s[h] = jnp.exp2(m_prev - m_new) * acc_s[h] + jnp.dot(
                p.astype(BF16), va_s[h, pl.ds(off, tk), :], preferred_element_type=F32)

    def two_full_steps(j, carry):
        step(2 * j, False)
        step(2 * j + 1, False)
        return carry

    lax.fori_loop(0, qi // 2, two_full_steps, 0)

    @pl.when(qi % 2 == 1)
    def _():
        step(qi - 1, False)

    step(qi, True)
    o0 = acc_s[0] / acc_s[0][:, HEAD_DIM:HEAD_DIM + 1]
    o1 = acc_s[1] / acc_s[1][:, HEAD_DIM:HEAD_DIM + 1]
    o_ref[...] = jnp.where(is_head, o0, pltpu.roll(o1, HEAD_DIM, 1)).astype(o_ref.dtype)


def _attention(q, k, v, ct, bsz, s):
    tq = min(ATT_TILE, s)
    nq = s // tq
    npair = N_HEADS // 2
    qspec = pl.BlockSpec((tq, LANES), lambda b, p, i: (b * nq + i, p))
    kvspec = pl.BlockSpec((s, LANES), lambda b, p, i: (b, p))
    return pl.pallas_call(
        _attn_body,
        grid=(bsz, npair, nq),
        in_specs=[qspec, kvspec, kvspec,
                  pl.BlockSpec((None, None, 2, tq), lambda b, p, i: (b, p, 0, i)),
                  pl.BlockSpec((None, None, 2, s), lambda b, p, i: (b, p, 0, 0))],
        out_specs=qspec,
        out_shape=jax.ShapeDtypeStruct((bsz * s, D_MODEL), BF16),
        scratch_shapes=[pltpu.VMEM((2, s, LANES), BF16), pltpu.VMEM((2, s, LANES), BF16),
                        pltpu.VMEM((2, tq, LANES), BF16), pltpu.VMEM((2, tq, LANES), F32),
                        pltpu.VMEM((2, tq, LANES), F32)],
        compiler_params=_params(("arbitrary", "arbitrary", "arbitrary")),
        name="attn",
    )(q, k, v, ct, ct)


def _merge_body(x_ref, ml_ref, yf_ref, gf_ref, wfo_ref, wo_ref, g_ref, wr_ref, br_ref,
                x2_ref, h2_ref, mi_ref, mw_ref, cnt_ref, running):
    tm = x_ref.shape[0]

    @pl.when(pl.program_id(0) == 0)
    def _():
        running[...] = jnp.zeros_like(running)

    fox = jnp.dot(yf_ref[...], wfo_ref[...], preferred_element_type=F32)
    merged = ml_ref[...].astype(F32) + jax.nn.sigmoid(gf_ref[...].astype(F32)) * fox
    x2 = x_ref[...] + jnp.dot(merged.astype(BF16), wo_ref[...], preferred_element_type=F32)
    x2_ref[...] = x2
    h2 = _rms(x2, g_ref[...])
    for c in range(ROW_SLABS):
        h2_ref[pl.ds(c, tm, stride=ROW_SLABS), :] = h2[:, c * LANES:(c + 1) * LANES]

    h_hi = h2.astype(BF16)
    h_lo = (h2 - h_hi.astype(F32)).astype(BF16)
    logits = (jnp.dot(h_hi, wr_ref[0], preferred_element_type=F32)
              + jnp.dot(h_lo, wr_ref[0], preferred_element_type=F32)
              + jnp.dot(h_hi, wr_ref[1], preferred_element_type=F32)) + br_ref[...]
    lane = lax.broadcasted_iota(I32, (tm, LANES), 1)
    lane_f = lane.astype(F32)
    logits = jnp.where(lane < N_EXPERTS, logits, -jnp.inf)
    vals, idxs = [], []
    for _ in range(TOP_K):
        mx = jnp.max(logits, axis=1, keepdims=True)
        ix = jnp.min(jnp.where(logits == mx, lane_f, float(LANES)), axis=1, keepdims=True).astype(I32)
        vals.append(mx)
        idxs.append(ix)
        logits = jnp.where(lane == ix, -jnp.inf, logits)
    exps = [jnp.exp(vk - vals[0]) for vk in vals]
    denom = exps[0] + exps[1] + exps[2] + exps[3]

    onehot = jnp.zeros((tm, LANES), F32)
    for ix in idxs:
        onehot = onehot + (lane == ix).astype(F32)
    r = lax.broadcasted_iota(I32, (tm, tm), 0)
    c = lax.broadcasted_iota(I32, (tm, tm), 1)
    before = jnp.dot((c < r).astype(BF16), onehot.astype(BF16), preferred_element_type=F32)
    pos = before + running[0:1, :]
    mi = jnp.zeros((tm, LANES), I32)
    mw = jnp.zeros((tm, LANES), F32)
    for kk in range(TOP_K):
        rank = jnp.sum(jnp.where(lane == idxs[kk], pos, 0.0), axis=1, keepdims=True).astype(I32)
        mi = jnp.where(lane == kk, idxs[kk], mi)
        mi = jnp.where(lane == TOP_K + kk, rank, mi)
        mw = jnp.where(lane == kk, exps[kk] / denom, mw)
    mi_ref[...] = mi
    mw_ref[...] = mw
    total = running[0:1, :] + jnp.sum(onehot, axis=0, keepdims=True)
    running[...] = jnp.broadcast_to(total, running.shape)
    cnt_ref[...] = jnp.broadcast_to(total, cnt_ref.shape).astype(I32)


def _merge(x2d, ml, yf, gf, w_fo, w_o, g_ffn, w_r, b_r):
    t = x2d.shape[0]
    tm = min(TOK_TILE, t)
    row = lambda i: (i, 0)
    tile = pl.BlockSpec((tm, D_MODEL), row)
    meta = pl.BlockSpec((tm, LANES), row)
    return pl.pallas_call(
        _merge_body,
        grid=(t // tm,),
        in_specs=[tile, tile, tile, tile, _const_spec(w_fo.shape), _const_spec(w_o.shape),
                  _const_spec((1, D_MODEL)), _const_spec(w_r.shape), _const_spec((1, LANES))],
        out_specs=[tile, pl.BlockSpec((tm * ROW_SLABS, LANES), row), meta, meta,
                   _const_spec((SUBLANES, LANES))],
        out_shape=[jax.ShapeDtypeStruct((t, D_MODEL), F32),
                   jax.ShapeDtypeStruct((t * ROW_SLABS, LANES), F32),
                   jax.ShapeDtypeStruct((t, LANES), I32),
                   jax.ShapeDtypeStruct((t, LANES), F32),
                   jax.ShapeDtypeStruct((SUBLANES, LANES), I32)],
        scratch_shapes=[pltpu.VMEM((SUBLANES, LANES), F32)],
        compiler_params=_params(("arbitrary",)),
        name="merge",
    )(x2d, ml, yf, gf, w_fo, w_o, g_ffn.reshape(1, D_MODEL), w_r, b_r)


def _row_slab(ref, row):
    return ref.at[pl.ds(pl.multiple_of(row * ROW_SLABS, ROW_SLABS), ROW_SLABS), :]


def _scatter_body(dest_ref, h2_ref, xs_in_hbm, xs_hbm, sem):
    del xs_in_hbm
    tb = dest_ref.shape[-1] // TOP_K

    def issue(j, carry):
        for kk in range(TOP_K):
            d = dest_ref[0, 0, j * TOP_K + kk]
            pltpu.make_async_copy(_row_slab(h2_ref, j), _row_slab(xs_hbm, d), sem).start(
                priority=kk % 2)
        return carry

    lax.fori_loop(0, tb, issue, 0)
    n = tb * TOP_K * ROW_SLABS
    pltpu.make_async_copy(xs_hbm.at[pl.ds(0, n), :], xs_hbm.at[pl.ds(0, n), :], sem).wait()


def _scatter(dest, h2s, n_rows):
    t = dest.shape[0]
    tb = min(ROW_TILE, t)
    nb = t // tb
    dest3 = dest.reshape(nb, 1, tb * TOP_K)
    zeros = jnp.zeros((n_rows * ROW_SLABS, LANES), F32)
    return pl.pallas_call(
        _scatter_body,
        grid=(nb,),
        in_specs=[pl.BlockSpec((1, 1, tb * TOP_K), lambda i: (i, 0, 0), memory_space=pltpu.SMEM),
                  pl.BlockSpec((tb * ROW_SLABS, LANES), lambda i: (i, 0)),
                  pl.BlockSpec(memory_space=pl.ANY)],
        out_specs=pl.BlockSpec(memory_space=pl.ANY),
        out_shape=jax.ShapeDtypeStruct((n_rows * ROW_SLABS, LANES), F32),
        scratch_shapes=[pltpu.SemaphoreType.DMA(())],
        input_output_aliases={2: 0},
        compiler_params=_params(("arbitrary",)),
        name="scatter",
    )(dest3, h2s, zeros)


def _expert_body(te_ref, first_ref, slot_ref, next_ref, nu_ref, xs_ref, wgu_hbm, bgu_ref, wd_hbm, bd_ref,
                 ys_ref, wgu_f, wd_f, wgu_bf, wd_bf, sems):
    i = pl.program_id(0)
    tm = xs_ref.shape[0] // ROW_SLABS

    def weight_copies(e, s):
        return (pltpu.make_async_copy(wgu_hbm.at[e], wgu_f.at[s], sems.at[0, s]),
                pltpu.make_async_copy(wd_hbm.at[e], wd_f.at[s], sems.at[1, s]))

    @pl.when(i == 0)
    def _():
        for cp in weight_copies(te_ref[0], 0):
            cp.start()

    @pl.when(first_ref[i] == 1)
    def _():
        s = slot_ref[i]
        for cp in weight_copies(te_ref[i], s):
            cp.wait()

        @pl.when(next_ref[i] >= 0)
        def _():
            for cp in weight_copies(next_ref[i], 1 - s):
                cp.start()

        wgu_bf[...] = wgu_f[s].astype(BF16)
        wd_bf[...] = wd_f[s].astype(BF16)

    @pl.when(i < nu_ref[0])
    def _():
        x = jnp.concatenate([xs_ref[pl.ds(c, tm, stride=ROW_SLABS), :] for c in range(ROW_SLABS)],
                            axis=1).astype(BF16)
        gu = jnp.dot(x, wgu_bf[...], preferred_element_type=F32) + bgu_ref[...]
        gate = jnp.minimum(gu[:, :D_FF], SWIGLU_LIMIT)
        up = jnp.clip(gu[:, D_FF:], -SWIGLU_LIMIT, SWIGLU_LIMIT)
        glu = gate * jax.nn.sigmoid(SWIGLU_ALPHA * gate)
        y = jnp.dot(((up + 1.0) * glu).astype(BF16), wd_bf[...],
                    preferred_element_type=F32) + bd_ref[...]
        for c in range(ROW_SLABS):
            ys_ref[pl.ds(c, tm, stride=ROW_SLABS), :] = y[:, c * LANES:(c + 1) * LANES]


def _experts(tile_expert, n_used, tile_end, xs, w_gate_up, b_gate_up, w_down, b_down, n_tiles):
    tm = EXP_TILE
    tile = jnp.arange(n_tiles, dtype=I32)
    used = tile < n_used[0]
    prev = jnp.concatenate([jnp.full((1,), -1, I32), tile_expert[:-1]])
    first = jnp.logical_and(used, tile_expert != prev).astype(I32)
    slot = ((jnp.cumsum(first) - 1) % 2).astype(I32)
    next_tile = tile_end[tile_expert].astype(I32)
    nxt = jnp.where(next_tile < n_used[0], tile_expert[jnp.minimum(next_tile, n_tiles - 1)], -1).astype(I32)

    rows = lambda i, te, fi, sl, nx, nu: (jnp.minimum(i, nu[0] - 1), 0)
    bsel = lambda i, te, fi, sl, nx, nu: (te[i], 0, 0)
    grid_spec = pltpu.PrefetchScalarGridSpec(
        num_scalar_prefetch=5,
        grid=(n_tiles,),
        in_specs=[pl.BlockSpec((tm * ROW_SLABS, LANES), rows),
                  pl.BlockSpec(memory_space=pl.ANY),
                  pl.BlockSpec((None, 1, 2 * D_FF), bsel),
                  pl.BlockSpec(memory_space=pl.ANY),
                  pl.BlockSpec((None, 1, D_MODEL), bsel)],
        out_specs=pl.BlockSpec((tm * ROW_SLABS, LANES), rows),
        scratch_shapes=[pltpu.VMEM((2, D_MODEL, 2 * D_FF), F32), pltpu.VMEM((2, D_FF, D_MODEL), F32),
                        pltpu.VMEM((D_MODEL, 2 * D_FF), BF16), pltpu.VMEM((D_FF, D_MODEL), BF16),
                        pltpu.SemaphoreType.DMA((2, 2))],
    )
    return pl.pallas_call(
        _expert_body,
        grid_spec=grid_spec,
        out_shape=jax.ShapeDtypeStruct(xs.shape, F32),
        input_output_aliases={5: 0},
        compiler_params=_params(("arbitrary",)),
        name="experts",
    )(tile_expert, first, slot, nxt, n_used, xs, w_gate_up, b_gate_up.reshape(N_EXPERTS, 1, 2 * D_FF),
      w_down, b_down.reshape(N_EXPERTS, 1, D_MODEL))


def _combine_body(dest_ref, x2_ref, mw_ref, g_ref, ys_hbm, o_ref, buf, sem):
    tb = x2_ref.shape[0]

    def issue(j, carry):
        for kk in range(TOP_K):
            d = dest_ref[0, 0, j * TOP_K + kk]
            pltpu.make_async_copy(_row_slab(ys_hbm, d), _row_slab(buf, kk * tb + j), sem).start(
                priority=kk % 2)
        return carry

    lax.fori_loop(0, tb, issue, 0)
    pltpu.make_async_copy(buf, buf, sem).wait()

    mw = mw_ref[...]
    cols = []
    for c in range(ROW_SLABS):
        acc = x2_ref[:, c * LANES:(c + 1) * LANES]
        for kk in range(TOP_K):
            rows = buf[pl.ds(kk * tb * ROW_SLABS + c, tb, stride=ROW_SLABS), :]
            acc = acc + mw[:, kk:kk + 1] * rows
        cols.append(acc)
    o_ref[...] = _rms(jnp.concatenate(cols, axis=1), g_ref[...])


def _combine(dest, x2, mw, g_final, ys):
    t = x2.shape[0]
    tb = min(ROW_TILE, t)
    nb = t // tb
    dest3 = dest.reshape(nb, 1, tb * TOP_K)
    row = lambda i: (i, 0)
    return pl.pallas_call(
        _combine_body,
        grid=(nb,),
        in_specs=[pl.BlockSpec((1, 1, tb * TOP_K), lambda i: (i, 0, 0), memory_space=pltpu.SMEM),
                  pl.BlockSpec((tb, D_MODEL), row), pl.BlockSpec((tb, LANES), row),
                  _const_spec((1, D_MODEL)), pl.BlockSpec(memory_space=pl.ANY)],
        out_specs=pl.BlockSpec((tb, D_MODEL), row),
        out_shape=jax.ShapeDtypeStruct((t, D_MODEL), F32),
        scratch_shapes=[pltpu.VMEM((TOP_K * tb * ROW_SLABS, LANES), F32), pltpu.SemaphoreType.DMA(())],
        compiler_params=_params(("arbitrary",)),
        name="combine",
    )(dest3, x2, mw, g_final.reshape(1, D_MODEL), ys)


def _block_diag_gates(w_rg, w_ig):
    per = GATE_CHUNK // RNN_BLOCK_DIM
    nchunk = w_rg.shape[0] // per
    eye = jnp.eye(per, dtype=w_rg.dtype)

    def bd(w):
        w = w.reshape(nchunk, per, RNN_BLOCK_DIM, RNN_BLOCK_DIM)
        return jnp.einsum('cpij,pq->cpiqj', w, eye).reshape(nchunk, GATE_CHUNK, GATE_CHUNK)

    return jnp.concatenate([bd(w_rg), bd(w_ig)], axis=-1).astype(BF16)


def kernel(x, g_mix, w_in, conv_w, conv_b, w_rg, b_rg, w_ig, b_ig, lru_lambda, b_f, w_lru_out,
           w_fox_out, w_o, g_ffn, w_router, b_router, w_gate_up, b_gate_up, w_down, b_down, g_final):
    bsz, s, d = x.shape
    assert d == D_MODEL
    t = bsz * s
    x2d = x.reshape(t, d)

    o_f = 5 * D_MODEL
    o_g = o_f + N_HEADS
    w_wide = jnp.concatenate([w_in[:, :o_f], w_in[:, o_g:]], axis=1).astype(BF16)
    w_f = jnp.pad(w_in[:, o_f:o_g], ((0, 0), (0, LANES - N_HEADS))).astype(BF16)
    b_f_pad = jnp.pad(b_f, (0, LANES - N_HEADS)).reshape(1, LANES)
    w_r = jnp.pad(w_router, ((0, 0), (0, LANES - N_EXPERTS)))
    w_r_hi = w_r.astype(BF16)
    w_r = jnp.stack([w_r_hi, (w_r - w_r_hi.astype(F32)).astype(BF16)])
    b_r = jnp.pad(b_router, (0, LANES - N_EXPERTS)).reshape(1, LANES)

    lx, lg, q, k, v, gl, gf, f_logit = _inproj(x2d, g_mix, w_wide, w_f)

    ml = _lru(lx, lg, gl, conv_w, conv_b, _block_diag_gates(w_rg, w_ig), b_rg, b_ig, lru_lambda,
              w_lru_out.astype(BF16), bsz, s)

    cum = _cumsum(f_logit, b_f_pad, bsz, s)
    ct = cum[:, :N_HEADS].reshape(bsz, s, N_HEADS // 2, 2).transpose(0, 2, 3, 1)
    yf = _attention(q, k, v, ct, bsz, s)

    x2, h2s, mi, mw, cnt = _merge(x2d, ml, yf, gf, w_fox_out.astype(BF16), w_o.astype(BF16),
                                  g_ffn, w_r, b_r)

    idx = mi[:, :TOP_K]
    rank = mi[:, TOP_K:2 * TOP_K]
    counts = cnt[0, :N_EXPERTS]
    tiles_e = (counts + EXP_TILE - 1) // EXP_TILE
    tile_end = jnp.cumsum(tiles_e)
    row_off = (tile_end - tiles_e) * EXP_TILE
    dest = (row_off[idx] + rank).astype(I32)
    n_tiles = (t * TOP_K) // EXP_TILE + N_EXPERTS
    n_used = tile_end[-1:].astype(I32)
    tile_ids = jnp.minimum(jnp.arange(n_tiles, dtype=I32), n_used[0] - 1)
    tile_expert = jnp.sum(tile_ids[:, None] >= tile_end[None, :], axis=1).astype(I32)

    xs = _scatter(dest, h2s, n_tiles * EXP_TILE)
    ys = _experts(tile_expert, n_used, tile_end, xs, w_gate_up, b_gate_up, w_down, b_down, n_tiles)
    out = _combine(dest, x2, mw, g_final, ys)
    return out.reshape(bsz, s, d)
```

```python
import functools
import math

import jax
import jax.numpy as jnp
from jax import lax
from jax.experimental import pallas as pl
from jax.experimental.pallas import tpu as pltpu

F32 = jnp.float32
BF16 = jnp.bfloat16
I32 = jnp.int32

D_MODEL = 1024
RNN_BLOCK_DIM = 64
LRU_C = 8.0
HEAD_DIM = 64
N_HEADS = 16
N_EXPERTS = 32
TOP_K = 4
D_FF = 1024
SWIGLU_LIMIT = 7.0
SWIGLU_ALPHA = 1.702
RMS_EPS = 1e-6
LOG2E = 1.4426950408889634
Q_SCALE = LOG2E / math.sqrt(HEAD_DIM)
Q_GROUP = 2

LANES = 128
SUBLANES = 8
ROW_SLABS = D_MODEL // LANES
VMEM_LIMIT_BYTES = 56 * 1024 * 1024

TOK_TILE = 256
LRU_TILE = 256
ATT_TILE = 512
EXP_TILE = 256
ROW_TILE = 256
CONV_WIDTH = 4
GATE_CHUNK = 256


def _params(semantics):
    return pltpu.CompilerParams(dimension_semantics=semantics, vmem_limit_bytes=VMEM_LIMIT_BYTES)


def _rms(x, g):
    return x * lax.rsqrt(jnp.mean(x * x, axis=-1, keepdims=True) + RMS_EPS) * g


def _const_spec(shape):
    nd = len(shape)
    return pl.BlockSpec(shape, lambda *_: (0,) * nd)


def _inproj_body(x_ref, g_ref, w_ref, wf_ref, *outs):
    h = _rms(x_ref[...], g_ref[...]).astype(BF16)
    *wide, o_f = outs
    for j, o in enumerate(wide):
        acc = jnp.dot(h, w_ref[:, j * D_MODEL:(j + 1) * D_MODEL], preferred_element_type=F32)
        if j == Q_GROUP:
            acc = acc * Q_SCALE
        o[...] = acc.astype(o.dtype)
    o_f[...] = jnp.dot(h, wf_ref[...], preferred_element_type=F32)


def _inproj(x2d, g_mix, w_wide, w_f):
    t = x2d.shape[0]
    n_wide = w_wide.shape[1] // D_MODEL
    tm = min(TOK_TILE, t)
    row = lambda i: (i, 0)
    return pl.pallas_call(
        _inproj_body,
        grid=(t // tm,),
        in_specs=[pl.BlockSpec((tm, D_MODEL), row), _const_spec((1, D_MODEL)),
                  _const_spec(w_wide.shape), _const_spec(w_f.shape)],
        out_specs=[pl.BlockSpec((tm, D_MODEL), row)] * n_wide + [pl.BlockSpec((tm, LANES), row)],
        out_shape=[jax.ShapeDtypeStruct((t, D_MODEL), BF16)] * n_wide
                  + [jax.ShapeDtypeStruct((t, LANES), F32)],
        compiler_params=_params(("arbitrary",)),
        name="inproj",
    )(x2d, g_mix.reshape(1, D_MODEL), w_wide, w_f)


def _lru_body(lx_ref, lg_ref, gl_ref, cw_ref, cb_ref, wbd_ref, brg_ref, big_ref, lam_ref, wo_ref,
              o_ref, xbuf, hcar, a_s, b_s, h_s):
    ts = lx_ref.shape[0]

    @pl.when(pl.program_id(1) == 0)
    def _():
        xbuf[0:SUBLANES, :] = jnp.zeros((SUBLANES, D_MODEL), F32)
        hcar[...] = jnp.zeros_like(hcar)

    x = lx_ref[...].astype(F32)
    xbuf[SUBLANES:SUBLANES + ts, :] = x
    xc = cb_ref[...] + cw_ref[CONV_WIDTH - 1:CONV_WIDTH, :] * x
    for k in range(CONV_WIDTH - 1):
        off = SUBLANES - (CONV_WIDTH - 1) + k
        xc = xc + cw_ref[k:k + 1, :] * xbuf[off:off + ts, :]
    xbuf[0:SUBLANES, :] = xbuf[ts:ts + SUBLANES, :]

    xcb = xc.astype(BF16)
    for c in range(D_MODEL // GATE_CHUNK):
        sl = slice(c * GATE_CHUNK, (c + 1) * GATE_CHUNK)
        g = jnp.dot(xcb[:, sl], wbd_ref[c], preferred_element_type=F32)
        rt = jax.nn.sigmoid(g[:, :GATE_CHUNK] + brg_ref[:, sl])
        it = jax.nn.sigmoid(g[:, GATE_CHUNK:] + big_ref[:, sl])
        nl = -lam_ref[:, sl]
        softplus = jnp.maximum(nl, 0.0) + jnp.log1p(jnp.exp(-jnp.abs(nl)))
        a = jnp.exp(-LRU_C * rt * softplus)
        a_s[:, sl] = a
        b_s[:, sl] = jnp.sqrt(1.0 - a * a) * (it * xc[:, sl])

    row = lax.broadcasted_iota(I32, (SUBLANES, D_MODEL), 0)

    def group(gidx, h0):
        off = pl.multiple_of(gidx * SUBLANES, SUBLANES)
        av = a_s[pl.ds(off, SUBLANES), :]
        bv = b_s[pl.ds(off, SUBLANES), :]
        for d in (1, 2, 4):
            keep = row >= d
            a_sh = jnp.where(keep, pltpu.roll(av, d, 0), 1.0)
            b_sh = jnp.where(keep, pltpu.roll(bv, d, 0), 0.0)
            bv = av * b_sh + bv
            av = av * a_sh
        hv = av * h0 + bv
        h_s[pl.ds(off, SUBLANES), :] = hv
        return jnp.broadcast_to(hv[SUBLANES - 1:SUBLANES, :], (SUBLANES, D_MODEL))

    hcar[...] = lax.fori_loop(0, ts // SUBLANES, group, hcar[...])

    y = (h_s[...] * jax.nn.gelu(lg_ref[...].astype(F32), approximate=True)).astype(BF16)
    proj = jnp.dot(y, wo_ref[...], preferred_element_type=F32)
    o_ref[...] = (jax.nn.sigmoid(gl_ref[...].astype(F32)) * proj).astype(o_ref.dtype)


def _lru(lx, lg, gl, conv_w, conv_b, wbd, b_rg, b_ig, lam, w_out, bsz, s):
    ts = min(LRU_TILE, s)
    ns = s // ts
    tile = pl.BlockSpec((ts, D_MODEL), lambda b, i: (b * ns + i, 0))
    vec = lambda a: a.reshape(1, D_MODEL)
    return pl.pallas_call(
        _lru_body,
        grid=(bsz, ns),
        in_specs=[tile, tile, tile, _const_spec((CONV_WIDTH, D_MODEL)), _const_spec((1, D_MODEL)),
                  _const_spec(wbd.shape), _const_spec((1, D_MODEL)), _const_spec((1, D_MODEL)),
                  _const_spec((1, D_MODEL)), _const_spec(w_out.shape)],
        out_specs=tile,
        out_shape=jax.ShapeDtypeStruct((bsz * s, D_MODEL), BF16),
        scratch_shapes=[pltpu.VMEM((ts + SUBLANES, D_MODEL), F32), pltpu.VMEM((SUBLANES, D_MODEL), F32),
                        pltpu.VMEM((ts, D_MODEL), F32), pltpu.VMEM((ts, D_MODEL), F32),
                        pltpu.VMEM((ts, D_MODEL), F32)],
        compiler_params=_params(("arbitrary", "arbitrary")),
        name="lru",
    )(lx, lg, gl, conv_w, vec(conv_b), wbd, vec(b_rg), vec(b_ig), vec(lam), w_out)


def _cumsum_body(f_ref, bf_ref, o_ref, carry):
    tc = f_ref.shape[0]

    @pl.when(pl.program_id(1) == 0)
    def _():
        carry[...] = jnp.zeros_like(carry)

    z = f_ref[...] + bf_ref[...]
    logf = jnp.minimum(z, 0.0) - jnp.log1p(jnp.exp(-jnp.abs(z)))
    r = lax.broadcasted_iota(I32, (tc, tc), 0)
    c = lax.broadcasted_iota(I32, (tc, tc), 1)
    tri = (r >= c).astype(F32)
    cum = jnp.dot(tri, logf, preferred_element_type=F32, precision=lax.Precision.HIGHEST)
    cum = cum + carry[0:1, :]
    o_ref[...] = cum
    carry[...] = jnp.broadcast_to(cum[tc - 1:tc, :], carry.shape)


def _cumsum(f_logit, b_f_pad, bsz, s):
    tc = min(TOK_TILE, s)
    ns = s // tc
    tile = pl.BlockSpec((tc, LANES), lambda b, i: (b * ns + i, 0))
    return pl.pallas_call(
        _cumsum_body,
        grid=(bsz, ns),
        in_specs=[tile, _const_spec((1, LANES))],
        out_specs=tile,
        out_shape=jax.ShapeDtypeStruct((bsz * s, LANES), F32),
        scratch_shapes=[pltpu.VMEM((SUBLANES, LANES), F32)],
        compiler_params=_params(("arbitrary", "arbitrary")),
        name="cumsum",
    )(f_logit, b_f_pad)


def _col_form(row_ref, h, start, n):
    blocks = []
    for j in range(n // LANES):
        r = row_ref[h:h + 1, pl.ds(start + j * LANES, LANES)]
        blocks.append(jnp.broadcast_to(r, (LANES, LANES)).T)
    return jnp.concatenate(blocks, axis=0)


def _split3(c):
    hi = c.astype(BF16).astype(F32)
    rest = c - hi
    mid = rest.astype(BF16).astype(F32)
    return hi, mid, rest - mid


def _attn_body(q_ref, k_ref, v_ref, cq_ref, ck_ref, o_ref, ka_s, va_s, qa_s, m_s, acc_s):
    tq = q_ref.shape[0]
    tk = tq
    s_len = k_ref.shape[0]
    qi = pl.program_id(2)
    lane = lax.broadcasted_iota(I32, (1, LANES), 1)
    is_head = lane < HEAD_DIM

    @pl.when(qi == 0)
    def _():
        def chunk(ci, carry):
            off = pl.multiple_of(ci * tk, tk)
            k2 = k_ref[pl.ds(off, tk), :].astype(F32)
            v2 = v_ref[pl.ds(off, tk), :].astype(F32)
            for h in range(2):
                kh = k2 if h == 0 else pltpu.roll(k2, HEAD_DIM, 1)
                vh = v2 if h == 0 else pltpu.roll(v2, HEAD_DIM, 1)
                hi, mid, lo = _split3(_col_form(ck_ref, h, off, tk) * LOG2E)
                ext = jnp.where(lane < HEAD_DIM + 3, 1.0,
                                jnp.where(lane == HEAD_DIM + 3, -hi,
                                          jnp.where(lane == HEAD_DIM + 4, -mid,
                                                    jnp.where(lane == HEAD_DIM + 5, -lo, 0.0))))
                ka_s[h, pl.ds(off, tk), :] = jnp.where(is_head, kh, ext).astype(BF16)
                ones_col = jnp.where(lane == HEAD_DIM, 1.0, 0.0)
                va_s[h, pl.ds(off, tk), :] = jnp.where(is_head, vh, ones_col).astype(BF16)
            return carry

        lax.fori_loop(0, s_len // tk, chunk, 0)

    q2 = q_ref[...].astype(F32)
    for h in range(2):
        qh = q2 if h == 0 else pltpu.roll(q2, HEAD_DIM, 1)
        hi, mid, lo = _split3(_col_form(cq_ref, h, 0, tq) * LOG2E)
        ext = jnp.where(lane == HEAD_DIM, hi,
                        jnp.where(lane == HEAD_DIM + 1, mid,
                                  jnp.where(lane == HEAD_DIM + 2, lo,
                                            jnp.where(lane < HEAD_DIM + 6, 1.0, 0.0))))
        qa_s[h] = jnp.where(is_head, qh, ext).astype(BF16)
    m_s[...] = jnp.full_like(m_s, -jnp.inf)
    acc_s[...] = jnp.zeros_like(acc_s)

    def step(ki, masked):
        off = pl.multiple_of(ki * tk, tk)
        for h in range(2):
            sc = lax.dot_general(qa_s[h], ka_s[h, pl.ds(off, tk), :], (((1,), (1,)), ((), ())),
                                 preferred_element_type=F32)
            if masked:
                r = lax.broadcasted_iota(I32, (tq, tk), 0)
                c = lax.broadcasted_iota(I32, (tq, tk), 1)
                sc = jnp.where(c <= r, sc, -jnp.inf)
            m_prev = m_s[h]
            m_new = jnp.maximum(m_prev, jnp.max(sc, axis=1, keepdims=True))
            p = jnp.concatenate([jnp.exp2(sc[:, j * LANES:(j + 1) * LANES] - m_new)
                                 for j in range(tk // LANES)], axis=1)
            m_s[h] = m_new
            acc_s[h] = jnp.exp2(m_prev - m_new) * acc_s[h] + jnp.dot(
                p.astype(BF16), va_s[h, pl.ds(off, tk), :], preferred_element_type=F32)

    def two_full_steps(j, carry):
        step(2 * j, False)
        step(2 * j + 1, False)
        return carry

    lax.fori_loop(0, qi // 2, two_full_steps, 0)

    @pl.when(qi % 2 == 1)
    def _():
        step(qi - 1, False)

    step(qi, True)
    o0 = acc_s[0] / acc_s[0][:, HEAD_DIM:HEAD_DIM + 1]
    o1 = acc_s[1] / acc_s[1][:, HEAD_DIM:HEAD_DIM + 1]
    o_ref[...] = jnp.where(is_head, o0, pltpu.roll(o1, HEAD_DIM, 1)).astype(o_ref.dtype)


def _attention(q, k, v, ct, bsz, s):
    tq = min(ATT_TILE, s)
    nq = s // tq
    npair = N_HEADS // 2
    qspec = pl.BlockSpec((tq, LANES), lambda b, p, i: (b * nq + i, p))
    kvspec = pl.BlockSpec((s, LANES), lambda b, p, i: (b, p))
    return pl.pallas_call(
        _attn_body,
        grid=(bsz, npair, nq),
        in_specs=[qspec, kvspec, kvspec,
                  pl.BlockSpec((None, None, 2, tq), lambda b, p, i: (b, p, 0, i)),
                  pl.BlockSpec((None, None, 2, s), lambda b, p, i: (b, p, 0, 0))],
        out_specs=qspec,
        out_shape=jax.ShapeDtypeStruct((bsz * s, D_MODEL), BF16),
        scratch_shapes=[pltpu.VMEM((2, s, LANES), BF16), pltpu.VMEM((2, s, LANES), BF16),
                        pltpu.VMEM((2, tq, LANES), BF16), pltpu.VMEM((2, tq, LANES), F32),
                        pltpu.VMEM((2, tq, LANES), F32)],
        compiler_params=_params(("arbitrary", "arbitrary", "arbitrary")),
        name="attn",
    )(q, k, v, ct, ct)


def _merge_body(x_ref, ml_ref, yf_ref, gf_ref, wfo_ref, wo_ref, g_ref, wr_ref, br_ref,
                x2_ref, h2_ref, mi_ref, mw_ref, cnt_ref, running):
    tm = x_ref.shape[0]

    @pl.when(pl.program_id(0) == 0)
    def _():
        running[...] = jnp.zeros_like(running)

    fox = jnp.dot(yf_ref[...], wfo_ref[...], preferred_element_type=F32)
    merged = ml_ref[...].astype(F32) + jax.nn.sigmoid(gf_ref[...].astype(F32)) * fox
    x2 = x_ref[...] + jnp.dot(merged.astype(BF16), wo_ref[...], preferred_element_type=F32)
    x2_ref[...] = x2
    h2 = _rms(x2, g_ref[...])
    for c in range(ROW_SLABS):
        h2_ref[pl.ds(c, tm, stride=ROW_SLABS), :] = h2[:, c * LANES:(c + 1) * LANES]

    h_hi = h2.astype(BF16)
    h_lo = (h2 - h_hi.astype(F32)).astype(BF16)
    logits = (jnp.dot(h_hi, wr_ref[0], preferred_element_type=F32)
              + jnp.dot(h_lo, wr_ref[0], preferred_element_type=F32)
              + jnp.dot(h_hi, wr_ref[1], preferred_element_type=F32)) + br_ref[...]
    lane = lax.broadcasted_iota(I32, (tm, LANES), 1)
    lane_f = lane.astype(F32)
    logits = jnp.where(lane < N_EXPERTS, logits, -jnp.inf)
    vals, idxs = [], []
    for _ in range(TOP_K):
        mx = jnp.max(logits, axis=1, keepdims=True)
        ix = jnp.min(jnp.where(logits == mx, lane_f, float(LANES)), axis=1, keepdims=True).astype(I32)
        vals.append(mx)
        idxs.append(ix)
        logits = jnp.where(lane == ix, -jnp.inf, logits)
    exps = [jnp.exp(vk - vals[0]) for vk in vals]
    denom = exps[0] + exps[1] + exps[2] + exps[3]

    onehot = jnp.zeros((tm, LANES), F32)
    for ix in idxs:
        onehot = onehot + (lane == ix).astype(F32)
    r = lax.broadcasted_iota(I32, (tm, tm), 0)
    c = lax.broadcasted_iota(I32, (tm, tm), 1)
    before = jnp.dot((c < r).astype(BF16), onehot.astype(BF16), preferred_element_type=F32)
    pos = before + running[0:1, :]
    mi = jnp.zeros((tm, LANES), I32)
    mw = jnp.zeros((tm, LANES), F32)
    for kk in range(TOP_K):
        rank = jnp.sum(jnp.where(lane == idxs[kk], pos, 0.0), axis=1, keepdims=True).astype(I32)
        mi = jnp.where(lane == kk, idxs[kk], mi)
        mi = jnp.where(lane == TOP_K + kk, rank, mi)
        mw = jnp.where(lane == kk, exps[kk] / denom, mw)
    mi_ref[...] = mi
    mw_ref[...] = mw
    total = running[0:1, :] + jnp.sum(onehot, axis=0, keepdims=True)
    running[...] = jnp.broadcast_to(total, running.shape)
    cnt_ref[...] = jnp.broadcast_to(total, cnt_ref.shape).astype(I32)


def _merge(x2d, ml, yf, gf, w_fo, w_o, g_ffn, w_r, b_r):
    t = x2d.shape[0]
    tm = min(TOK_TILE, t)
    row = lambda i: (i, 0)
    tile = pl.BlockSpec((tm, D_MODEL), row)
    meta = pl.BlockSpec((tm, LANES), row)
    return pl.pallas_call(
        _merge_body,
        grid=(t // tm,),
        in_specs=[tile, tile, tile, tile, _const_spec(w_fo.shape), _const_spec(w_o.shape),
                  _const_spec((1, D_MODEL)), _const_spec(w_r.shape), _const_spec((1, LANES))],
        out_specs=[tile, pl.BlockSpec((tm * ROW_SLABS, LANES), row), meta, meta,
                   _const_spec((SUBLANES, LANES))],
        out_shape=[jax.ShapeDtypeStruct((t, D_MODEL), F32),
                   jax.ShapeDtypeStruct((t * ROW_SLABS, LANES), F32),
                   jax.ShapeDtypeStruct((t, LANES), I32),
                   jax.ShapeDtypeStruct((t, LANES), F32),
                   jax.ShapeDtypeStruct((SUBLANES, LANES), I32)],
        scratch_shapes=[pltpu.VMEM((SUBLANES, LANES), F32)],
        compiler_params=_params(("arbitrary",)),
        name="merge",
    )(x2d, ml, yf, gf, w_fo, w_o, g_ffn.reshape(1, D_MODEL), w_r, b_r)


def _row_slab(ref, row, base=0):
    return ref.at[pl.ds(pl.multiple_of(base + row * ROW_SLABS, ROW_SLABS), ROW_SLABS), :]


def _expert_body(te_ref, first_ref, slot_ref, next_ref, nu_ref, src0_ref, src_ref, dst_ref, h2_hbm, wgu_hbm,
                 bgu_ref, wd_hbm, bd_ref, g_hbm, wgu_f, wd_f, wgu_bf, wd_bf, x0, x1, y0, y1, wsem, isem, osem):
    i = pl.program_id(0)
    tm = EXP_TILE
    nu = nu_ref[0]

    def weight_copies(e, s):
        return (pltpu.make_async_copy(wgu_hbm.at[e], wgu_f.at[s], wsem.at[0, s]),
                pltpu.make_async_copy(wd_hbm.at[e], wd_f.at[s], wsem.at[1, s]))

    def x_slot_copy(buf, s):
        return pltpu.make_async_copy(h2_hbm.at[pl.ds(0, tm * ROW_SLABS), :], buf, isem.at[s])

    def y_slot_copy(buf, s):
        return pltpu.make_async_copy(buf, g_hbm.at[pl.ds(0, tm * ROW_SLABS), :], osem.at[s])

    @pl.when(i == 0)
    def _():
        for cp in weight_copies(te_ref[0], 0):
            cp.start()
        y1[...] = jnp.zeros_like(y1)
        for j in range(tm):
            pltpu.make_async_copy(_row_slab(h2_hbm, src0_ref[0, 0, j]), _row_slab(x0, j),
                                  isem.at[0]).start(priority=j % 2)

    def tile_step(cur, x_cur, x_oth, y_cur, y_oth):
        oth = 1 - cur
        x_slot_copy(x_cur, cur).wait()

        @pl.when(i >= 1)
        def _():
            y_slot_copy(y_cur, cur).wait()

        for j in range(tm):
            pltpu.make_async_copy(_row_slab(h2_hbm, src_ref[0, 0, j]), _row_slab(x_oth, j),
                                  isem.at[oth]).start(priority=0)
            pltpu.make_async_copy(_row_slab(y_oth, j), _row_slab(g_hbm, dst_ref[0, 0, j]),
                                  osem.at[oth]).start(priority=1)

        x = jnp.concatenate([x_cur[pl.ds(c, tm, stride=ROW_SLABS), :] for c in range(ROW_SLABS)],
                            axis=1).astype(BF16)
        gu = jnp.dot(x, wgu_bf[...], preferred_element_type=F32) + bgu_ref[...]
        gate = jnp.minimum(gu[:, :D_FF], SWIGLU_LIMIT)
        up = jnp.clip(gu[:, D_FF:], -SWIGLU_LIMIT, SWIGLU_LIMIT)
        glu = gate * jax.nn.sigmoid(SWIGLU_ALPHA * gate)
        y = jnp.dot(((up + 1.0) * glu).astype(BF16), wd_bf[...],
                    preferred_element_type=F32) + bd_ref[...]
        for c in range(ROW_SLABS):
            y_cur[pl.ds(c, tm, stride=ROW_SLABS), :] = y[:, c * LANES:(c + 1) * LANES]

        @pl.when(i == nu)
        def _():
            x_slot_copy(x_oth, oth).wait()
            y_slot_copy(y_oth, oth).wait()

    @pl.when(i <= nu)
    def _():
        @pl.when(first_ref[i] == 1)
        def _():
            s = slot_ref[i]
            for cp in weight_copies(te_ref[i], s):
                cp.wait()

            @pl.when(next_ref[i] >= 0)
            def _():
                for cp in weight_copies(next_ref[i], 1 - s):
                    cp.start()

            wgu_bf[...] = wgu_f[s].astype(BF16)
            wd_bf[...] = wd_f[s].astype(BF16)

        @pl.when(i % 2 == 0)
        def _():
            tile_step(0, x0, x1, y0, y1)

        @pl.when(i % 2 == 1)
        def _():
            tile_step(1, x1, x0, y1, y0)


def _experts(tile_expert, n_used, tile_end, src_tab, dst_tab, h2s, w_gate_up, b_gate_up, w_down, b_down,
             n_tiles, n_out_rows):
    tm = EXP_TILE
    tile = jnp.arange(n_tiles + 1, dtype=I32)
    used = tile < n_used[0]
    te = jnp.concatenate([tile_expert, tile_expert[-1:]])
    prev = jnp.concatenate([jnp.full((1,), -1, I32), te[:-1]])
    first = jnp.logical_and(used, te != prev).astype(I32)
    slot = ((jnp.cumsum(first) - 1) % 2).astype(I32)
    next_tile = tile_end[te].astype(I32)
    nxt = jnp.where(next_tile < n_used[0], te[jnp.minimum(next_tile, n_tiles - 1)], -1).astype(I32)

    smem = lambda imap: pl.BlockSpec((1, 1, tm), imap, memory_space=pltpu.SMEM)
    bsel = lambda i, te, fi, sl, nx, nu: (te[jnp.minimum(i, nu[0] - 1)], 0, 0)
    grid_spec = pltpu.PrefetchScalarGridSpec(
        num_scalar_prefetch=5,
        grid=(n_tiles + 1,),
        in_specs=[smem(lambda i, te, fi, sl, nx, nu: (0, 0, 0)),
                  smem(lambda i, te, fi, sl, nx, nu: (jnp.minimum(i + 1, nu[0] - 1), 0, 0)),
                  smem(lambda i, te, fi, sl, nx, nu: (i, 0, 0)),
                  pl.BlockSpec(memory_space=pl.ANY),
                  pl.BlockSpec(memory_space=pl.ANY),
                  pl.BlockSpec((None, 1, 2 * D_FF), bsel),
                  pl.BlockSpec(memory_space=pl.ANY),
                  pl.BlockSpec((None, 1, D_MODEL), bsel)],
        out_specs=pl.BlockSpec(memory_space=pl.ANY),
        scratch_shapes=[pltpu.VMEM((2, D_MODEL, 2 * D_FF), F32), pltpu.VMEM((2, D_FF, D_MODEL), F32),
                        pltpu.VMEM((D_MODEL, 2 * D_FF), BF16), pltpu.VMEM((D_FF, D_MODEL), BF16),
                        pltpu.VMEM((tm * ROW_SLABS, LANES), F32), pltpu.VMEM((tm * ROW_SLABS, LANES), F32),
                        pltpu.VMEM((tm * ROW_SLABS, LANES), F32), pltpu.VMEM((tm * ROW_SLABS, LANES), F32),
                        pltpu.SemaphoreType.DMA((2, 2)), pltpu.SemaphoreType.DMA((2,)),
                        pltpu.SemaphoreType.DMA((2,))],
    )
    return pl.pallas_call(
        _expert_body,
        grid_spec=grid_spec,
        out_shape=jax.ShapeDtypeStruct((n_out_rows * ROW_SLABS, LANES), F32),
        compiler_params=_params(("arbitrary",)),
        name="experts",
    )(te, first, slot, nxt, n_used, src_tab, src_tab, dst_tab, h2s, w_gate_up,
      b_gate_up.reshape(N_EXPERTS, 1, 2 * D_FF), w_down, b_down.reshape(N_EXPERTS, 1, D_MODEL))


def _combine_body(x2_ref, mw_ref, g_ref, y0_ref, y1_ref, y2_ref, y3_ref, o_ref):
    tb = x2_ref.shape[0]
    mw = mw_ref[...]
    wk = [jnp.broadcast_to(mw[:, kk:kk + 1], (tb, LANES)) for kk in range(TOP_K)]
    cols = []
    for c in range(ROW_SLABS):
        acc = x2_ref[:, c * LANES:(c + 1) * LANES]
        for kk, y_ref in enumerate((y0_ref, y1_ref, y2_ref, y3_ref)):
            acc = acc + wk[kk] * y_ref[pl.ds(c, tb, stride=ROW_SLABS), :]
        cols.append(acc)
    o_ref[...] = _rms(jnp.concatenate(cols, axis=1), g_ref[...])


def _combine(x2, mw, g_final, ys):
    t = x2.shape[0]
    tb = min(ROW_TILE, t)
    nb = t // tb
    row = lambda i: (i, 0)
    yspec = lambda kk: pl.BlockSpec((tb * ROW_SLABS, LANES), lambda i: (kk * nb + i, 0))
    return pl.pallas_call(
        _combine_body,
        grid=(nb,),
        in_specs=[pl.BlockSpec((tb, D_MODEL), row), pl.BlockSpec((tb, LANES), row),
                  _const_spec((1, D_MODEL))] + [yspec(kk) for kk in range(TOP_K)],
        out_specs=pl.BlockSpec((tb, D_MODEL), row),
        out_shape=jax.ShapeDtypeStruct((t, D_MODEL), F32),
        compiler_params=_params(("arbitrary",)),
        name="combine",
    )(x2, mw, g_final.reshape(1, D_MODEL), ys, ys, ys, ys)


def _block_diag_gates(w_rg, w_ig):
    per = GATE_CHUNK // RNN_BLOCK_DIM
    nchunk = w_rg.shape[0] // per
    eye = jnp.eye(per, dtype=w_rg.dtype)

    def bd(w):
        w = w.reshape(nchunk, per, RNN_BLOCK_DIM, RNN_BLOCK_DIM)
        return jnp.einsum('cpij,pq->cpiqj', w, eye).reshape(nchunk, GATE_CHUNK, GATE_CHUNK)

    return jnp.concatenate([bd(w_rg), bd(w_ig)], axis=-1).astype(BF16)


def kernel(x, g_mix, w_in, conv_w, conv_b, w_rg, b_rg, w_ig, b_ig, lru_lambda, b_f, w_lru_out,
           w_fox_out, w_o, g_ffn, w_router, b_router, w_gate_up, b_gate_up, w_down, b_down, g_final):
    bsz, s, d = x.shape
    assert d == D_MODEL
    t = bsz * s
    x2d = x.reshape(t, d)

    o_f = 5 * D_MODEL
    o_g = o_f + N_HEADS
    w_wide = jnp.concatenate([w_in[:, :o_f], w_in[:, o_g:]], axis=1).astype(BF16)
    w_f = jnp.pad(w_in[:, o_f:o_g], ((0, 0), (0, LANES - N_HEADS))).astype(BF16)
    b_f_pad = jnp.pad(b_f, (0, LANES - N_HEADS)).reshape(1, LANES)
    w_r = jnp.pad(w_router, ((0, 0), (0, LANES - N_EXPERTS)))
    w_r_hi = w_r.astype(BF16)
    w_r = jnp.stack([w_r_hi, (w_r - w_r_hi.astype(F32)).astype(BF16)])
    b_r = jnp.pad(b_router, (0, LANES - N_EXPERTS)).reshape(1, LANES)

    lx, lg, q, k, v, gl, gf, f_logit = _inproj(x2d, g_mix, w_wide, w_f)

    ml = _lru(lx, lg, gl, conv_w, conv_b, _block_diag_gates(w_rg, w_ig), b_rg, b_ig, lru_lambda,
              w_lru_out.astype(BF16), bsz, s)

    cum = _cumsum(f_logit, b_f_pad, bsz, s)
    ct = cum[:, :N_HEADS].reshape(bsz, s, N_HEADS // 2, 2).transpose(0, 2, 3, 1)
    yf = _attention(q, k, v, ct, bsz, s)

    x2, h2s, mi, mw, cnt = _merge(x2d, ml, yf, gf, w_fox_out.astype(BF16), w_o.astype(BF16),
                                  g_ffn, w_r, b_r)

    idx = mi[:, :TOP_K]
    rank = mi[:, TOP_K:2 * TOP_K]
    counts = cnt[0, :N_EXPERTS]
    tiles_e = (counts + EXP_TILE - 1) // EXP_TILE
    tile_end = jnp.cumsum(tiles_e)
    row_off = (tile_end - tiles_e) * EXP_TILE
    dest = (row_off[idx] + rank).astype(I32)
    n_tiles = (t * TOP_K) // EXP_TILE + N_EXPERTS
    n_used = tile_end[-1:].astype(I32)
    tile_ids = jnp.minimum(jnp.arange(n_tiles, dtype=I32), n_used[0] - 1)
    tile_expert = jnp.sum(tile_ids[:, None] >= tile_end[None, :], axis=1).astype(I32)

    pair_of_row = jnp.full((n_tiles * EXP_TILE,), -1, I32).at[dest.reshape(-1)].set(
        jnp.arange(t * TOP_K, dtype=I32), unique_indices=True)
    routed = pair_of_row >= 0
    spare = TOP_K * t + jnp.arange(EXP_TILE, dtype=I32)
    src_tab = jnp.where(routed, pair_of_row // TOP_K, 0).reshape(n_tiles, 1, EXP_TILE)
    dst_rows = jnp.where(routed, (pair_of_row % TOP_K) * t + pair_of_row // TOP_K,
                         jnp.tile(spare, n_tiles))
    dst_tab = jnp.concatenate([spare, dst_rows]).reshape(n_tiles + 1, 1, EXP_TILE)

    ys = _experts(tile_expert, n_used, tile_end, src_tab, dst_tab, h2s, w_gate_up, b_gate_up, w_down,
                  b_down, n_tiles, TOP_K * t + EXP_TILE)
    out = _combine(x2, mw, g_final, ys)
    return out.reshape(bsz, s, d)
```

```python
import functools
import math

import jax
import jax.numpy as jnp
from jax import lax
from jax.experimental import pallas as pl
from jax.experimental.pallas import tpu as pltpu

F32 = jnp.float32
BF16 = jnp.bfloat16
I32 = jnp.int32

D_MODEL = 1024
RNN_BLOCK_DIM = 64
LRU_C = 8.0
HEAD_DIM = 64
N_HEADS = 16
N_EXPERTS = 32
TOP_K = 4
D_FF = 1024
SWIGLU_LIMIT = 7.0
SWIGLU_ALPHA = 1.702
RMS_EPS = 1e-6
LOG2E = 1.4426950408889634
Q_SCALE = LOG2E / math.sqrt(HEAD_DIM)
Q_GROUP = 2

LANES = 128
SUBLANES = 8
ROW_SLABS = D_MODEL // LANES
VMEM_LIMIT_BYTES = 56 * 1024 * 1024

TOK_TILE = 256
LRU_TILE = 256
ATT_TILE = 512
EXP_TILE = 256
ROW_TILE = 256
CONV_WIDTH = 4
GATE_CHUNK = 256


def _params(semantics):
    return pltpu.CompilerParams(dimension_semantics=semantics, vmem_limit_bytes=VMEM_LIMIT_BYTES)


def _rms(x, g):
    return x * lax.rsqrt(jnp.mean(x * x, axis=-1, keepdims=True) + RMS_EPS) * g


def _const_spec(shape):
    nd = len(shape)
    return pl.BlockSpec(shape, lambda *_: (0,) * nd)


def _inproj_body(x_ref, g_ref, w_ref, wf_ref, bf_ref, *rest, steps_per_seq):
    *wide, o_cum, carry = rest
    tm = x_ref.shape[0]

    @pl.when(pl.program_id(0) % steps_per_seq == 0)
    def _():
        carry[...] = jnp.zeros_like(carry)

    h = _rms(x_ref[...], g_ref[...]).astype(BF16)
    for j, o in enumerate(wide):
        acc = jnp.dot(h, w_ref[:, j * D_MODEL:(j + 1) * D_MODEL], preferred_element_type=F32)
        if j == Q_GROUP:
            acc = acc * Q_SCALE
        o[...] = acc.astype(o.dtype)

    z = jnp.dot(h, wf_ref[...], preferred_element_type=F32) + bf_ref[...]
    logf = jnp.minimum(z, 0.0) - jnp.log1p(jnp.exp(-jnp.abs(z)))
    r = lax.broadcasted_iota(I32, (tm, tm), 0)
    c = lax.broadcasted_iota(I32, (tm, tm), 1)
    tri = (r >= c).astype(BF16)
    cum = carry[0:1, :]
    for term in _split3(logf):
        cum = cum + jnp.dot(tri, term.astype(BF16), preferred_element_type=F32)
    o_cum[...] = cum
    carry[...] = jnp.broadcast_to(cum[tm - 1:tm, :], carry.shape)


def _inproj(x2d, g_mix, w_wide, w_f, b_f_pad, s):
    t = x2d.shape[0]
    n_wide = w_wide.shape[1] // D_MODEL
    tm = min(TOK_TILE, s)
    row = lambda i: (i, 0)
    return pl.pallas_call(
        functools.partial(_inproj_body, steps_per_seq=s // tm),
        grid=(t // tm,),
        in_specs=[pl.BlockSpec((tm, D_MODEL), row), _const_spec((1, D_MODEL)),
                  _const_spec(w_wide.shape), _const_spec(w_f.shape), _const_spec((1, LANES))],
        out_specs=[pl.BlockSpec((tm, D_MODEL), row)] * n_wide + [pl.BlockSpec((tm, LANES), row)],
        out_shape=[jax.ShapeDtypeStruct((t, D_MODEL), BF16)] * n_wide
                  + [jax.ShapeDtypeStruct((t, LANES), F32)],
        scratch_shapes=[pltpu.VMEM((SUBLANES, LANES), F32)],
        compiler_params=_params(("arbitrary",)),
        name="inproj",
    )(x2d, g_mix.reshape(1, D_MODEL), w_wide, w_f, b_f_pad)


def _lru_body(lx_ref, lg_ref, gl_ref, cw_ref, cb_ref, wbd_ref, brg_ref, big_ref, lam_ref, wo_ref,
              o_ref, xbuf, hcar, a_s, b_s, h_s):
    ts = lx_ref.shape[0]

    @pl.when(pl.program_id(1) == 0)
    def _():
        xbuf[0:SUBLANES, :] = jnp.zeros((SUBLANES, D_MODEL), F32)
        hcar[...] = jnp.zeros_like(hcar)

    x = lx_ref[...].astype(F32)
    xbuf[SUBLANES:SUBLANES + ts, :] = x
    xc = cb_ref[...] + cw_ref[CONV_WIDTH - 1:CONV_WIDTH, :] * x
    for k in range(CONV_WIDTH - 1):
        off = SUBLANES - (CONV_WIDTH - 1) + k
        xc = xc + cw_ref[k:k + 1, :] * xbuf[off:off + ts, :]
    xbuf[0:SUBLANES, :] = xbuf[ts:ts + SUBLANES, :]

    xcb = xc.astype(BF16)
    for c in range(D_MODEL // GATE_CHUNK):
        sl = slice(c * GATE_CHUNK, (c + 1) * GATE_CHUNK)
        g = jnp.dot(xcb[:, sl], wbd_ref[c], preferred_element_type=F32)
        rt = jax.nn.sigmoid(g[:, :GATE_CHUNK] + brg_ref[:, sl])
        it = jax.nn.sigmoid(g[:, GATE_CHUNK:] + big_ref[:, sl])
        nl = -lam_ref[:, sl]
        softplus = jnp.maximum(nl, 0.0) + jnp.log1p(jnp.exp(-jnp.abs(nl)))
        a = jnp.exp(-LRU_C * rt * softplus)
        a_s[:, sl] = a
        b_s[:, sl] = jnp.sqrt(1.0 - a * a) * (it * xc[:, sl])

    row = lax.broadcasted_iota(I32, (SUBLANES, D_MODEL), 0)

    def group(gidx, h0):
        off = pl.multiple_of(gidx * SUBLANES, SUBLANES)
        av = a_s[pl.ds(off, SUBLANES), :]
        bv = b_s[pl.ds(off, SUBLANES), :]
        for d in (1, 2, 4):
            keep = row >= d
            a_sh = jnp.where(keep, pltpu.roll(av, d, 0), 1.0)
            b_sh = jnp.where(keep, pltpu.roll(bv, d, 0), 0.0)
            bv = av * b_sh + bv
            av = av * a_sh
        hv = av * h0 + bv
        h_s[pl.ds(off, SUBLANES), :] = hv
        return jnp.broadcast_to(hv[SUBLANES - 1:SUBLANES, :], (SUBLANES, D_MODEL))

    hcar[...] = lax.fori_loop(0, ts // SUBLANES, group, hcar[...])

    y = (h_s[...] * jax.nn.gelu(lg_ref[...].astype(F32), approximate=True)).astype(BF16)
    proj = jnp.dot(y, wo_ref[...], preferred_element_type=F32)
    o_ref[...] = (jax.nn.sigmoid(gl_ref[...].astype(F32)) * proj).astype(o_ref.dtype)


def _lru(lx, lg, gl, conv_w, conv_b, wbd, b_rg, b_ig, lam, w_out, bsz, s):
    ts = min(LRU_TILE, s)
    ns = s // ts
    tile = pl.BlockSpec((ts, D_MODEL), lambda b, i: (b * ns + i, 0))
    vec = lambda a: a.reshape(1, D_MODEL)
    return pl.pallas_call(
        _lru_body,
        grid=(bsz, ns),
        in_specs=[tile, tile, tile, _const_spec((CONV_WIDTH, D_MODEL)), _const_spec((1, D_MODEL)),
                  _const_spec(wbd.shape), _const_spec((1, D_MODEL)), _const_spec((1, D_MODEL)),
                  _const_spec((1, D_MODEL)), _const_spec(w_out.shape)],
        out_specs=tile,
        out_shape=jax.ShapeDtypeStruct((bsz * s, D_MODEL), BF16),
        scratch_shapes=[pltpu.VMEM((ts + SUBLANES, D_MODEL), F32), pltpu.VMEM((SUBLANES, D_MODEL), F32),
                        pltpu.VMEM((ts, D_MODEL), F32), pltpu.VMEM((ts, D_MODEL), F32),
                        pltpu.VMEM((ts, D_MODEL), F32)],
        compiler_params=_params(("arbitrary", "arbitrary")),
        name="lru",
    )(lx, lg, gl, conv_w, vec(conv_b), wbd, vec(b_rg), vec(b_ig), vec(lam), w_out)


def _split3(c):
    hi = c.astype(BF16).astype(F32)
    rest = c - hi
    mid = rest.astype(BF16).astype(F32)
    return hi, mid, rest - mid


def _bias_columns(row_ref, h, start, n, query_side):
    hi, mid, lo = _split3(row_ref[h:h + 1, pl.ds(start, n)] * LOG2E)
    one = jnp.ones_like(hi)
    terms = (hi, mid, lo, one, one, one) if query_side else (one, one, one, -hi, -mid, -lo)
    sub = lax.broadcasted_iota(I32, (SUBLANES, n), 0)
    rows = jnp.zeros((SUBLANES, n), F32)
    for r, term in enumerate(terms):
        rows = jnp.where(sub == r, term, rows)
    above = jnp.zeros((HEAD_DIM, LANES), F32)
    below = jnp.zeros((LANES - HEAD_DIM - SUBLANES, LANES), F32)
    blocks = [jnp.concatenate([above, rows[:, j * LANES:(j + 1) * LANES], below], axis=0).T
              for j in range(n // LANES)]
    return jnp.concatenate(blocks, axis=0)


def _attn_body(q_ref, k_ref, v_ref, cq_ref, ck_ref, o_ref, ka_s, va_s, qa_s, m_s, acc_s):
    tq = q_ref.shape[0]
    tk = tq
    s_len = k_ref.shape[0]
    qi = pl.program_id(2)
    lane = lax.broadcasted_iota(I32, (1, LANES), 1)
    is_head = lane < HEAD_DIM
    ones_col = jnp.where(lane == HEAD_DIM, 1.0, 0.0)

    @pl.when(qi == 0)
    def _():
        def chunk(ci, carry):
            off = pl.multiple_of(ci * tk, tk)
            k2 = k_ref[pl.ds(off, tk), :].astype(F32)
            v2 = v_ref[pl.ds(off, tk), :].astype(F32)
            for h in range(2):
                kh = k2 if h == 0 else pltpu.roll(k2, HEAD_DIM, 1)
                vh = v2 if h == 0 else pltpu.roll(v2, HEAD_DIM, 1)
                ext = _bias_columns(ck_ref, h, off, tk, query_side=False)
                ka_s[h, pl.ds(off, tk), :] = jnp.where(is_head, kh, ext).astype(BF16)
                va_s[h, pl.ds(off, tk), :] = jnp.where(is_head, vh, ones_col).astype(BF16)
            return carry

        lax.fori_loop(0, s_len // tk, chunk, 0)

    q2 = q_ref[...].astype(F32)
    for h in range(2):
        qh = q2 if h == 0 else pltpu.roll(q2, HEAD_DIM, 1)
        ext = _bias_columns(cq_ref, h, 0, tq, query_side=True)
        qa_s[h] = jnp.where(is_head, qh, ext).astype(BF16)
    m_s[...] = jnp.full_like(m_s, -jnp.inf)
    acc_s[...] = jnp.zeros_like(acc_s)

    def step(ki, masked):
        off = pl.multiple_of(ki * tk, tk)
        for h in range(2):
            sc = lax.dot_general(qa_s[h], ka_s[h, pl.ds(off, tk), :], (((1,), (1,)), ((), ())),
                                 preferred_element_type=F32)
            if masked:
                r = lax.broadcasted_iota(I32, (tq, tk), 0)
                c = lax.broadcasted_iota(I32, (tq, tk), 1)
                sc = jnp.where(c <= r, sc, -jnp.inf)
            m_prev = m_s[h]
            m_new = jnp.maximum(m_prev, jnp.max(sc, axis=1, keepdims=True))
            p = jnp.concatenate([jnp.exp2(sc[:, j * LANES:(j + 1) * LANES] - m_new)
                                 for j in range(tk // LANES)], axis=1)
            m_s[h] = m_new
            acc_s[h] = jnp.exp2(m_prev - m_new) * acc_s[h] + jnp.dot(
                p.astype(BF16), va_s[h, pl.ds(off, tk), :], preferred_element_type=F32)

    def two_full_steps(j, carry):
        step(2 * j, False)
        step(2 * j + 1, False)
        return carry

    lax.fori_loop(0, qi // 2, two_full_steps, 0)

    @pl.when(qi % 2 == 1)
    def _():
        step(qi - 1, False)

    step(qi, True)
    o0 = acc_s[0] / acc_s[0][:, HEAD_DIM:HEAD_DIM + 1]
    o1 = acc_s[1] / acc_s[1][:, HEAD_DIM:HEAD_DIM + 1]
    o_ref[...] = jnp.where(is_head, o0, pltpu.roll(o1, HEAD_DIM, 1)).astype(o_ref.dtype)


def _attention(q, k, v, ct, bsz, s):
    tq = min(ATT_TILE, s)
    nq = s // tq
    npair = N_HEADS // 2
    qspec = pl.BlockSpec((tq, LANES), lambda b, p, i: (b * nq + i, p))
    kvspec = pl.BlockSpec((s, LANES), lambda b, p, i: (b, p))
    return pl.pallas_call(
        _attn_body,
        grid=(bsz, npair, nq),
        in_specs=[qspec, kvspec, kvspec,
                  pl.BlockSpec((None, None, 2, tq), lambda b, p, i: (b, p, 0, i)),
                  pl.BlockSpec((None, None, 2, s), lambda b, p, i: (b, p, 0, 0))],
        out_specs=qspec,
        out_shape=jax.ShapeDtypeStruct((bsz * s, D_MODEL), BF16),
        scratch_shapes=[pltpu.VMEM((2, s, LANES), BF16), pltpu.VMEM((2, s, LANES), BF16),
                        pltpu.VMEM((2, tq, LANES), BF16), pltpu.VMEM((2, tq, LANES), F32),
                        pltpu.VMEM((2, tq, LANES), F32)],
        compiler_params=_params(("arbitrary", "arbitrary", "arbitrary")),
        name="attn",
    )(q, k, v, ct, ct)


def _merge_body(x_ref, ml_ref, yf_ref, gf_ref, wfo_ref, wo_ref, g_ref, wr_ref, br_ref,
                x2_ref, h2_ref, mi_ref, mw_ref, cnt_ref, running):
    tm = x_ref.shape[0]

    @pl.when(pl.program_id(0) == 0)
    def _():
        running[...] = jnp.zeros_like(running)

    fox = jnp.dot(yf_ref[...], wfo_ref[...], preferred_element_type=F32)
    merged = ml_ref[...].astype(F32) + jax.nn.sigmoid(gf_ref[...].astype(F32)) * fox
    x2 = x_ref[...] + jnp.dot(merged.astype(BF16), wo_ref[...], preferred_element_type=F32)
    x2_ref[...] = x2
    h2 = _rms(x2, g_ref[...])
    for c in range(ROW_SLABS):
        h2_ref[pl.ds(c, tm, stride=ROW_SLABS), :] = h2[:, c * LANES:(c + 1) * LANES]

    h_hi = h2.astype(BF16)
    h_lo = (h2 - h_hi.astype(F32)).astype(BF16)
    logits = (jnp.dot(h_hi, wr_ref[0], preferred_element_type=F32)
              + jnp.dot(h_lo, wr_ref[0], preferred_element_type=F32)
              + jnp.dot(h_hi, wr_ref[1], preferred_element_type=F32)) + br_ref[...]
    lane = lax.broadcasted_iota(I32, (tm, LANES), 1)
    lane_f = lane.astype(F32)
    logits = jnp.where(lane < N_EXPERTS, logits, -jnp.inf)
    vals, idxs = [], []
    for _ in range(TOP_K):
        mx = jnp.max(logits, axis=1, keepdims=True)
        ix = jnp.min(jnp.where(logits == mx, lane_f, float(LANES)), axis=1, keepdims=True).astype(I32)
        vals.append(mx)
        idxs.append(ix)
        logits = jnp.where(lane == ix, -jnp.inf, logits)
    exps = [jnp.exp(vk - vals[0]) for vk in vals]
    denom = exps[0] + exps[1] + exps[2] + exps[3]

    onehot = jnp.zeros((tm, LANES), F32)
    for ix in idxs:
        onehot = onehot + (lane == ix).astype(F32)
    r = lax.broadcasted_iota(I32, (tm, tm), 0)
    c = lax.broadcasted_iota(I32, (tm, tm), 1)
    before = jnp.dot((c < r).astype(BF16), onehot.astype(BF16), preferred_element_type=F32)
    pos = before + running[0:1, :]
    mi = jnp.zeros((tm, LANES), F32)
    mw = jnp.zeros((tm, LANES), F32)
    for kk in range(TOP_K):
        rank = jnp.sum(jnp.where(lane == idxs[kk], pos, 0.0), axis=1, keepdims=True)
        mi = jnp.where(lane == kk, idxs[kk].astype(F32), mi)
        mi = jnp.where(lane == TOP_K + kk, rank, mi)
        mw = jnp.where(lane == kk, exps[kk] / denom, mw)
    mi_ref[...] = mi.T[0:2 * TOP_K, :].astype(I32)
    mw_ref[...] = mw
    total = running[0:1, :] + jnp.sum(onehot, axis=0, keepdims=True)
    running[...] = jnp.broadcast_to(total, running.shape)
    cnt_ref[...] = jnp.broadcast_to(total, cnt_ref.shape).astype(I32)


def _merge(x2d, ml, yf, gf, w_fo, w_o, g_ffn, w_r, b_r):
    t = x2d.shape[0]
    tm = min(TOK_TILE, t)
    row = lambda i: (i, 0)
    tile = pl.BlockSpec((tm, D_MODEL), row)
    meta = pl.BlockSpec((tm, LANES), row)
    return pl.pallas_call(
        _merge_body,
        grid=(t // tm,),
        in_specs=[tile, tile, tile, tile, _const_spec(w_fo.shape), _const_spec(w_o.shape),
                  _const_spec((1, D_MODEL)), _const_spec(w_r.shape), _const_spec((1, LANES))],
        out_specs=[tile, pl.BlockSpec((tm * ROW_SLABS, LANES), row),
                   pl.BlockSpec((None, 2 * TOP_K, tm), lambda i: (i, 0, 0)), meta,
                   _const_spec((SUBLANES, LANES))],
        out_shape=[jax.ShapeDtypeStruct((t, D_MODEL), F32),
                   jax.ShapeDtypeStruct((t * ROW_SLABS, LANES), F32),
                   jax.ShapeDtypeStruct((t // tm, 2 * TOP_K, tm), I32),
                   jax.ShapeDtypeStruct((t, LANES), F32),
                   jax.ShapeDtypeStruct((SUBLANES, LANES), I32)],
        scratch_shapes=[pltpu.VMEM((SUBLANES, LANES), F32)],
        compiler_params=_params(("arbitrary",)),
        name="merge",
    )(x2d, ml, yf, gf, w_fo, w_o, g_ffn.reshape(1, D_MODEL), w_r, b_r)


def _row_slab(ref, row, base=0):
    return ref.at[pl.ds(pl.multiple_of(base + row * ROW_SLABS, ROW_SLABS), ROW_SLABS), :]


def _scatter_body(pad_start_ref, pad_ref, nu_ref, dest_ref, h2_ref, xs_hbm, zbuf, sem, zsem, *, n_tiles):
    tb = dest_ref.shape[-1] // TOP_K
    tile_rows = EXP_TILE * ROW_SLABS

    @pl.when(pl.program_id(0) == 0)
    def _():
        zbuf[...] = jnp.zeros_like(zbuf)

        def pad_pieces():
            for e in range(N_EXPERTS):
                pad = pad_ref[e]
                for bit in reversed(range(EXP_TILE.bit_length() - 1)):
                    size = 1 << bit
                    before = (pad >> (bit + 1)) << (bit + 1)
                    first = pl.multiple_of((pad_start_ref[e] + before) * ROW_SLABS, ROW_SLABS)
                    cp = pltpu.make_async_copy(zbuf.at[pl.ds(0, size * ROW_SLABS), :],
                                               xs_hbm.at[pl.ds(first, size * ROW_SLABS), :], zsem)
                    yield (pad & size) != 0, cp

        def tail_copy(j):
            return pltpu.make_async_copy(
                zbuf, xs_hbm.at[pl.ds(pl.multiple_of(j * tile_rows, tile_rows), tile_rows), :], zsem)

        for cond, cp in pad_pieces():
            pl.when(cond)(cp.start)
        lax.fori_loop(nu_ref[0], n_tiles, lambda j, c: (tail_copy(j).start(), c)[1], 0)
        for cond, cp in pad_pieces():
            pl.when(cond)(cp.wait)
        lax.fori_loop(nu_ref[0], n_tiles, lambda j, c: (tail_copy(j).wait(), c)[1], 0)

    def issue(j, carry):
        for kk in range(TOP_K):
            d = dest_ref[0, 0, kk * tb + j]
            pltpu.make_async_copy(_row_slab(h2_ref, j), _row_slab(xs_hbm, d), sem).start(
                priority=kk % 2)
        return carry

    lax.fori_loop(0, tb, issue, 0)
    n = tb * TOP_K * ROW_SLABS
    pltpu.make_async_copy(xs_hbm.at[pl.ds(0, n), :], xs_hbm.at[pl.ds(0, n), :], sem).wait()


def _scatter(dest3, pad_start, pad, n_used, h2s, n_tiles):
    nb, _, n = dest3.shape
    tb = n // TOP_K
    grid_spec = pltpu.PrefetchScalarGridSpec(
        num_scalar_prefetch=3,
        grid=(nb,),
        in_specs=[pl.BlockSpec((1, 1, n), lambda i, *_: (i, 0, 0), memory_space=pltpu.SMEM),
                  pl.BlockSpec((tb * ROW_SLABS, LANES), lambda i, *_: (i, 0))],
        out_specs=pl.BlockSpec(memory_space=pl.ANY),
        scratch_shapes=[pltpu.VMEM((EXP_TILE * ROW_SLABS, LANES), F32), pltpu.SemaphoreType.DMA(()),
                        pltpu.SemaphoreType.DMA(())],
    )
    return pl.pallas_call(
        functools.partial(_scatter_body, n_tiles=n_tiles),
        grid_spec=grid_spec,
        out_shape=jax.ShapeDtypeStruct((n_tiles * EXP_TILE * ROW_SLABS, LANES), F32),
        compiler_params=_params(("arbitrary",)),
        name="scatter",
    )(pad_start, pad, n_used, dest3, h2s)


def _expert_body(te_ref, first_ref, slot_ref, next_ref, nu_ref, xs_ref, wgu_hbm, bgu_ref, wd_hbm, bd_ref,
                 ys_ref, wgu_f, wd_f, wgu_bf, wd_bf, sems):
    i = pl.program_id(0)
    tm = xs_ref.shape[0] // ROW_SLABS

    def weight_copies(e, s):
        return (pltpu.make_async_copy(wgu_hbm.at[e], wgu_f.at[s], sems.at[0, s]),
                pltpu.make_async_copy(wd_hbm.at[e], wd_f.at[s], sems.at[1, s]))

    @pl.when(i == 0)
    def _():
        for cp in weight_copies(te_ref[0], 0):
            cp.start()

    @pl.when(first_ref[i] == 1)
    def _():
        s = slot_ref[i]
        for cp in weight_copies(te_ref[i], s):
            cp.wait()

        @pl.when(next_ref[i] >= 0)
        def _():
            for cp in weight_copies(next_ref[i], 1 - s):
                cp.start()

        wgu_bf[...] = wgu_f[s].astype(BF16)
        wd_bf[...] = wd_f[s].astype(BF16)

    @pl.when(i < nu_ref[0])
    def _():
        x = jnp.concatenate([xs_ref[pl.ds(c, tm, stride=ROW_SLABS), :] for c in range(ROW_SLABS)],
                            axis=1).astype(BF16)
        gu = jnp.dot(x, wgu_bf[...], preferred_element_type=F32) + bgu_ref[...]
        gate = jnp.minimum(gu[:, :D_FF], SWIGLU_LIMIT)
        up = jnp.clip(gu[:, D_FF:], -SWIGLU_LIMIT, SWIGLU_LIMIT)
        glu = gate * jax.nn.sigmoid(SWIGLU_ALPHA * gate)
        y = jnp.dot(((up + 1.0) * glu).astype(BF16), wd_bf[...],
                    preferred_element_type=F32) + bd_ref[...]
        for c in range(ROW_SLABS):
            ys_ref[pl.ds(c, tm, stride=ROW_SLABS), :] = y[:, c * LANES:(c + 1) * LANES]


def _experts(tile_expert, n_used, tile_end, xs, w_gate_up, b_gate_up, w_down, b_down, n_tiles):
    tm = EXP_TILE
    tile = jnp.arange(n_tiles, dtype=I32)
    used = tile < n_used[0]
    prev = jnp.concatenate([jnp.full((1,), -1, I32), tile_expert[:-1]])
    first = jnp.logical_and(used, tile_expert != prev).astype(I32)
    slot = ((jnp.cumsum(first) - 1) % 2).astype(I32)
    next_tile = tile_end[tile_expert].astype(I32)
    nxt = jnp.where(next_tile < n_used[0], tile_expert[jnp.minimum(next_tile, n_tiles - 1)], -1).astype(I32)

    rows = lambda i, te, fi, sl, nx, nu: (jnp.minimum(i, nu[0] - 1), 0)
    bsel = lambda i, te, fi, sl, nx, nu: (te[i], 0, 0)
    grid_spec = pltpu.PrefetchScalarGridSpec(
        num_scalar_prefetch=5,
        grid=(n_tiles,),
        in_specs=[pl.BlockSpec((tm * ROW_SLABS, LANES), rows),
                  pl.BlockSpec(memory_space=pl.ANY),
                  pl.BlockSpec((None, 1, 2 * D_FF), bsel),
                  pl.BlockSpec(memory_space=pl.ANY),
                  pl.BlockSpec((None, 1, D_MODEL), bsel)],
        out_specs=pl.BlockSpec((tm * ROW_SLABS, LANES), rows),
        scratch_shapes=[pltpu.VMEM((2, D_MODEL, 2 * D_FF), F32), pltpu.VMEM((2, D_FF, D_MODEL), F32),
                        pltpu.VMEM((D_MODEL, 2 * D_FF), BF16), pltpu.VMEM((D_FF, D_MODEL), BF16),
                        pltpu.SemaphoreType.DMA((2, 2))],
    )
    return pl.pallas_call(
        _expert_body,
        grid_spec=grid_spec,
        out_shape=jax.ShapeDtypeStruct(xs.shape, F32),
        input_output_aliases={5: 0},
        compiler_params=_params(("arbitrary",)),
        name="experts",
    )(tile_expert, first, slot, nxt, n_used, xs, w_gate_up, b_gate_up.reshape(N_EXPERTS, 1, 2 * D_FF),
      w_down, b_down.reshape(N_EXPERTS, 1, D_MODEL))


def _combine_body(dcur_ref, dnext_ref, x2_ref, mw_ref, g_ref, ys_hbm, o_ref, buf, sem):
    i = pl.program_id(0)
    tb = x2_ref.shape[0]
    slot_rows = TOP_K * tb * ROW_SLABS
    cur = i % 2

    def issue(d_ref, slot):
        base = pl.multiple_of(slot * slot_rows, slot_rows)

        def body(j, carry):
            for kk in range(TOP_K):
                d = d_ref[0, 0, kk * tb + j]
                pltpu.make_async_copy(_row_slab(ys_hbm, d), _row_slab(buf, kk * tb + j, base),
                                      sem.at[slot]).start(priority=kk % 2)
            return carry

        lax.fori_loop(0, tb, body, 0)

    @pl.when(i == 0)
    def _():
        issue(dcur_ref, 0)

    @pl.when(i + 1 < pl.num_programs(0))
    def _():
        issue(dnext_ref, 1 - cur)

    base = pl.multiple_of(cur * slot_rows, slot_rows)
    pltpu.make_async_copy(ys_hbm.at[pl.ds(0, slot_rows), :], buf.at[pl.ds(base, slot_rows), :],
                          sem.at[cur]).wait()

    mw = mw_ref[...]
    wk = [jnp.broadcast_to(mw[:, kk:kk + 1], (tb, LANES)) for kk in range(TOP_K)]
    cols = []
    for c in range(ROW_SLABS):
        acc = x2_ref[:, c * LANES:(c + 1) * LANES]
        for kk in range(TOP_K):
            rows = buf[pl.ds(base + kk * tb * ROW_SLABS + c, tb, stride=ROW_SLABS), :]
            acc = acc + wk[kk] * rows
        cols.append(acc)
    o_ref[...] = _rms(jnp.concatenate(cols, axis=1), g_ref[...])


def _combine(dest3, x2, mw, g_final, ys):
    t = x2.shape[0]
    nb, _, n = dest3.shape
    tb = n // TOP_K
    row = lambda i: (i, 0)
    dspec = lambda imap: pl.BlockSpec((1, 1, n), imap, memory_space=pltpu.SMEM)
    return pl.pallas_call(
        _combine_body,
        grid=(nb,),
        in_specs=[dspec(lambda i: (i, 0, 0)), dspec(lambda i: (jnp.minimum(i + 1, nb - 1), 0, 0)),
                  pl.BlockSpec((tb, D_MODEL), row), pl.BlockSpec((tb, LANES), row),
                  _const_spec((1, D_MODEL)), pl.BlockSpec(memory_space=pl.ANY)],
        out_specs=pl.BlockSpec((tb, D_MODEL), row),
        out_shape=jax.ShapeDtypeStruct((t, D_MODEL), F32),
        scratch_shapes=[pltpu.VMEM((2 * TOP_K * tb * ROW_SLABS, LANES), F32),
                        pltpu.SemaphoreType.DMA((2,))],
        compiler_params=_params(("arbitrary",)),
        name="combine",
    )(dest3, dest3, x2, mw, g_final.reshape(1, D_MODEL), ys)


def _block_diag_gates(w_rg, w_ig):
    per = GATE_CHUNK // RNN_BLOCK_DIM
    nchunk = w_rg.shape[0] // per
    eye = jnp.eye(per, dtype=w_rg.dtype)

    def bd(w):
        w = w.reshape(nchunk, per, RNN_BLOCK_DIM, RNN_BLOCK_DIM)
        return jnp.einsum('cpij,pq->cpiqj', w, eye).reshape(nchunk, GATE_CHUNK, GATE_CHUNK)

    return jnp.concatenate([bd(w_rg), bd(w_ig)], axis=-1).astype(BF16)


def kernel(x, g_mix, w_in, conv_w, conv_b, w_rg, b_rg, w_ig, b_ig, lru_lambda, b_f, w_lru_out,
           w_fox_out, w_o, g_ffn, w_router, b_router, w_gate_up, b_gate_up, w_down, b_down, g_final):
    bsz, s, d = x.shape
    assert d == D_MODEL
    t = bsz * s
    x2d = x.reshape(t, d)

    o_f = 5 * D_MODEL
    o_g = o_f + N_HEADS
    w_wide = jnp.concatenate([w_in[:, :o_f], w_in[:, o_g:]], axis=1).astype(BF16)
    w_f = jnp.pad(w_in[:, o_f:o_g], ((0, 0), (0, LANES - N_HEADS))).astype(BF16)
    b_f_pad = jnp.pad(b_f, (0, LANES - N_HEADS)).reshape(1, LANES)
    w_r = jnp.pad(w_router, ((0, 0), (0, LANES - N_EXPERTS)))
    w_r_hi = w_r.astype(BF16)
    w_r = jnp.stack([w_r_hi, (w_r - w_r_hi.astype(F32)).astype(BF16)])
    b_r = jnp.pad(b_router, (0, LANES - N_EXPERTS)).reshape(1, LANES)

    lx, lg, q, k, v, gl, gf, cum = _inproj(x2d, g_mix, w_wide, w_f, b_f_pad, s)

    ml = _lru(lx, lg, gl, conv_w, conv_b, _block_diag_gates(w_rg, w_ig), b_rg, b_ig, lru_lambda,
              w_lru_out.astype(BF16), bsz, s)

    ct = cum[:, :N_HEADS].reshape(bsz, s, N_HEADS // 2, 2).transpose(0, 2, 3, 1)
    yf = _attention(q, k, v, ct, bsz, s)

    x2, h2s, route, mw, cnt = _merge(x2d, ml, yf, gf, w_fox_out.astype(BF16), w_o.astype(BF16),
                                     g_ffn, w_r, b_r)

    assert TOK_TILE == ROW_TILE
    counts = cnt[0, :N_EXPERTS]
    tiles_e = (counts + EXP_TILE - 1) // EXP_TILE
    tile_end = jnp.cumsum(tiles_e)
    row_off = (tile_end - tiles_e) * EXP_TILE
    dest3 = (row_off[route[:, :TOP_K, :]] + route[:, TOP_K:, :]).astype(I32).reshape(
        route.shape[0], 1, TOP_K * route.shape[2])
    n_tiles = (t * TOP_K) // EXP_TILE + N_EXPERTS
    n_used = tile_end[-1:].astype(I32)
    tile_ids = jnp.minimum(jnp.arange(n_tiles, dtype=I32), n_used[0] - 1)
    tile_expert = jnp.sum(tile_ids[:, None] >= tile_end[None, :], axis=1).astype(I32)

    xs = _scatter(dest3, (row_off + counts).astype(I32), (tiles_e * EXP_TILE - counts).astype(I32),
                  n_used, h2s, n_tiles)
    ys = _experts(tile_expert, n_used, tile_end, xs, w_gate_up, b_gate_up, w_down, b_down, n_tiles)
    out = _combine(dest3, x2, mw, g_final, ys)
    return out.reshape(bsz, s, d)
```

```python
import functools
import math

import jax
import jax.numpy as jnp
from jax import lax
from jax.experimental import pallas as pl
from jax.experimental.pallas import tpu as pltpu

F32 = jnp.float32
BF16 = jnp.bfloat16
I32 = jnp.int32

D_MODEL = 1024
RNN_BLOCK_DIM = 64
LRU_C = 8.0
HEAD_DIM = 64
N_HEADS = 16
N_EXPERTS = 32
TOP_K = 4
D_FF = 1024
SWIGLU_LIMIT = 7.0
SWIGLU_ALPHA = 1.702
RMS_EPS = 1e-6
LOG2E = 1.4426950408889634
Q_SCALE = LOG2E / math.sqrt(HEAD_DIM)
Q_GROUP = 2

LANES = 128
SUBLANES = 8
ROW_SLABS = D_MODEL // LANES
VMEM_LIMIT_BYTES = 56 * 1024 * 1024

TOK_TILE = 256
LRU_TILE = 256
ATT_TILE = 512
EXP_TILE = 256
ROW_TILE = 256
ISSUE_UNROLL = 4
CONV_WIDTH = 4
GATE_CHUNK = 256


def _params(semantics):
    return pltpu.CompilerParams(dimension_semantics=semantics, vmem_limit_bytes=VMEM_LIMIT_BYTES)


def _rms(x, g):
    return x * lax.rsqrt(jnp.mean(x * x, axis=-1, keepdims=True) + RMS_EPS) * g


def _const_spec(shape):
    nd = len(shape)
    return pl.BlockSpec(shape, lambda *_: (0,) * nd)


def _inproj_body(x_ref, g_ref, w_ref, wf_ref, bf_ref, *rest, steps_per_seq):
    *wide, o_cum, carry = rest
    tm = x_ref.shape[0]

    @pl.when(pl.program_id(0) % steps_per_seq == 0)
    def _():
        carry[...] = jnp.zeros_like(carry)

    h = _rms(x_ref[...], g_ref[...]).astype(BF16)
    for j, o in enumerate(wide):
        acc = jnp.dot(h, w_ref[:, j * D_MODEL:(j + 1) * D_MODEL], preferred_element_type=F32)
        if j == Q_GROUP:
            acc = acc * Q_SCALE
        o[...] = acc.astype(o.dtype)

    z = jnp.dot(h, wf_ref[...], preferred_element_type=F32) + bf_ref[...]
    logf = jnp.minimum(z, 0.0) - jnp.log1p(jnp.exp(-jnp.abs(z)))
    r = lax.broadcasted_iota(I32, (tm, tm), 0)
    c = lax.broadcasted_iota(I32, (tm, tm), 1)
    tri = (r >= c).astype(BF16)
    cum = carry[0:1, :]
    for term in _split3(logf):
        cum = cum + jnp.dot(tri, term.astype(BF16), preferred_element_type=F32)
    o_cum[...] = cum
    carry[...] = jnp.broadcast_to(cum[tm - 1:tm, :], carry.shape)


def _inproj(x2d, g_mix, w_wide, w_f, b_f_pad, s):
    t = x2d.shape[0]
    n_wide = w_wide.shape[1] // D_MODEL
    tm = min(TOK_TILE, s)
    row = lambda i: (i, 0)
    return pl.pallas_call(
        functools.partial(_inproj_body, steps_per_seq=s // tm),
        grid=(t // tm,),
        in_specs=[pl.BlockSpec((tm, D_MODEL), row), _const_spec((1, D_MODEL)),
                  _const_spec(w_wide.shape), _const_spec(w_f.shape), _const_spec((1, LANES))],
        out_specs=[pl.BlockSpec((tm, D_MODEL), row)] * n_wide + [pl.BlockSpec((tm, LANES), row)],
        out_shape=[jax.ShapeDtypeStruct((t, D_MODEL), BF16)] * n_wide
                  + [jax.ShapeDtypeStruct((t, LANES), F32)],
        scratch_shapes=[pltpu.VMEM((SUBLANES, LANES), F32)],
        compiler_params=_params(("arbitrary",)),
        name="inproj",
    )(x2d, g_mix.reshape(1, D_MODEL), w_wide, w_f, b_f_pad)


def _lru_body(lx_ref, lg_ref, gl_ref, cw_ref, cb_ref, wbd_ref, brg_ref, big_ref, lam_ref, wo_ref,
              o_ref, xbuf, hcar, a_s, b_s, h_s):
    ts = lx_ref.shape[0]

    @pl.when(pl.program_id(1) == 0)
    def _():
        xbuf[0:SUBLANES, :] = jnp.zeros((SUBLANES, D_MODEL), F32)
        hcar[...] = jnp.zeros_like(hcar)

    x = lx_ref[...].astype(F32)
    xbuf[SUBLANES:SUBLANES + ts, :] = x
    xc = cb_ref[...] + cw_ref[CONV_WIDTH - 1:CONV_WIDTH, :] * x
    for k in range(CONV_WIDTH - 1):
        off = SUBLANES - (CONV_WIDTH - 1) + k
        xc = xc + cw_ref[k:k + 1, :] * xbuf[off:off + ts, :]
    xbuf[0:SUBLANES, :] = xbuf[ts:ts + SUBLANES, :]

    xcb = xc.astype(BF16)
    for c in range(D_MODEL // GATE_CHUNK):
        sl = slice(c * GATE_CHUNK, (c + 1) * GATE_CHUNK)
        g = jnp.dot(xcb[:, sl], wbd_ref[c], preferred_element_type=F32)
        rt = jax.nn.sigmoid(g[:, :GATE_CHUNK] + brg_ref[:, sl])
        it = jax.nn.sigmoid(g[:, GATE_CHUNK:] + big_ref[:, sl])
        nl = -lam_ref[:, sl]
        softplus = jnp.maximum(nl, 0.0) + jnp.log1p(jnp.exp(-jnp.abs(nl)))
        a = jnp.exp(-LRU_C * rt * softplus)
        a_s[:, sl] = a
        b_s[:, sl] = jnp.sqrt(1.0 - a * a) * (it * xc[:, sl])

    row = lax.broadcasted_iota(I32, (SUBLANES, D_MODEL), 0)

    def group(gidx, h0):
        off = pl.multiple_of(gidx * SUBLANES, SUBLANES)
        av = a_s[pl.ds(off, SUBLANES), :]
        bv = b_s[pl.ds(off, SUBLANES), :]
        for d in (1, 2, 4):
            keep = row >= d
            a_sh = jnp.where(keep, pltpu.roll(av, d, 0), 1.0)
            b_sh = jnp.where(keep, pltpu.roll(bv, d, 0), 0.0)
            bv = av * b_sh + bv
            av = av * a_sh
        hv = av * h0 + bv
        h_s[pl.ds(off, SUBLANES), :] = hv
        return jnp.broadcast_to(hv[SUBLANES - 1:SUBLANES, :], (SUBLANES, D_MODEL))

    hcar[...] = lax.fori_loop(0, ts // SUBLANES, group, hcar[...])

    y = (h_s[...] * jax.nn.gelu(lg_ref[...].astype(F32), approximate=True)).astype(BF16)
    proj = jnp.dot(y, wo_ref[...], preferred_element_type=F32)
    o_ref[...] = (jax.nn.sigmoid(gl_ref[...].astype(F32)) * proj).astype(o_ref.dtype)


def _lru(lx, lg, gl, conv_w, conv_b, wbd, b_rg, b_ig, lam, w_out, bsz, s):
    ts = min(LRU_TILE, s)
    ns = s // ts
    tile = pl.BlockSpec((ts, D_MODEL), lambda b, i: (b * ns + i, 0))
    vec = lambda a: a.reshape(1, D_MODEL)
    return pl.pallas_call(
        _lru_body,
        grid=(bsz, ns),
        in_specs=[tile, tile, tile, _const_spec((CONV_WIDTH, D_MODEL)), _const_spec((1, D_MODEL)),
                  _const_spec(wbd.shape), _const_spec((1, D_MODEL)), _const_spec((1, D_MODEL)),
                  _const_spec((1, D_MODEL)), _const_spec(w_out.shape)],
        out_specs=tile,
        out_shape=jax.ShapeDtypeStruct((bsz * s, D_MODEL), BF16),
        scratch_shapes=[pltpu.VMEM((ts + SUBLANES, D_MODEL), F32), pltpu.VMEM((SUBLANES, D_MODEL), F32),
                        pltpu.VMEM((ts, D_MODEL), F32), pltpu.VMEM((ts, D_MODEL), F32),
                        pltpu.VMEM((ts, D_MODEL), F32)],
        compiler_params=_params(("arbitrary", "arbitrary")),
        name="lru",
    )(lx, lg, gl, conv_w, vec(conv_b), wbd, vec(b_rg), vec(b_ig), vec(lam), w_out)


def _split3(c):
    hi = c.astype(BF16).astype(F32)
    rest = c - hi
    mid = rest.astype(BF16).astype(F32)
    return hi, mid, rest - mid


def _bias_columns(row_ref, h, start, n, query_side):
    hi, mid, lo = _split3(row_ref[h:h + 1, pl.ds(start, n)] * LOG2E)
    one = jnp.ones_like(hi)
    terms = (hi, mid, lo, one, one, one) if query_side else (one, one, one, -hi, -mid, -lo)
    sub = lax.broadcasted_iota(I32, (SUBLANES, n), 0)
    rows = jnp.zeros((SUBLANES, n), F32)
    for r, term in enumerate(terms):
        rows = jnp.where(sub == r, term, rows)
    above = jnp.zeros((HEAD_DIM, LANES), F32)
    below = jnp.zeros((LANES - HEAD_DIM - SUBLANES, LANES), F32)
    blocks = [jnp.concatenate([above, rows[:, j * LANES:(j + 1) * LANES], below], axis=0).T
              for j in range(n // LANES)]
    return jnp.concatenate(blocks, axis=0)


def _attn_body(q_ref, k_ref, v_ref, cq_ref, ck_ref, o_ref, ka_s, va_s, qa_s, m_s, acc_s):
    tq = q_ref.shape[0]
    tk = tq
    s_len = k_ref.shape[0]
    qi = pl.program_id(2)
    lane = lax.broadcasted_iota(I32, (1, LANES), 1)
    is_head = lane < HEAD_DIM
    ones_col = jnp.where(lane == HEAD_DIM, 1.0, 0.0)

    @pl.when(qi == 0)
    def _():
        def chunk(ci, carry):
            off = pl.multiple_of(ci * tk, tk)
            k2 = k_ref[pl.ds(off, tk), :].astype(F32)
            v2 = v_ref[pl.ds(off, tk), :].astype(F32)
            for h in range(2):
                kh = k2 if h == 0 else pltpu.roll(k2, HEAD_DIM, 1)
                vh = v2 if h == 0 else pltpu.roll(v2, HEAD_DIM, 1)
                ext = _bias_columns(ck_ref, h, off, tk, query_side=False)
                ka_s[h, pl.ds(off, tk), :] = jnp.where(is_head, kh, ext).astype(BF16)
                va_s[h, pl.ds(off, tk), :] = jnp.where(is_head, vh, ones_col).astype(BF16)
            return carry

        lax.fori_loop(0, s_len // tk, chunk, 0)

    q2 = q_ref[...].astype(F32)
    for h in range(2):
        qh = q2 if h == 0 else pltpu.roll(q2, HEAD_DIM, 1)
        ext = _bias_columns(cq_ref, h, 0, tq, query_side=True)
        qa_s[h] = jnp.where(is_head, qh, ext).astype(BF16)
    m_s[...] = jnp.full_like(m_s, -jnp.inf)
    acc_s[...] = jnp.zeros_like(acc_s)

    def step(ki, masked):
        off = pl.multiple_of(ki * tk, tk)
        for h in range(2):
            sc = lax.dot_general(qa_s[h], ka_s[h, pl.ds(off, tk), :], (((1,), (1,)), ((), ())),
                                 preferred_element_type=F32)
            if masked:
                r = lax.broadcasted_iota(I32, (tq, tk), 0)
                c = lax.broadcasted_iota(I32, (tq, tk), 1)
                sc = jnp.where(c <= r, sc, -jnp.inf)
            m_prev = m_s[h]
            m_new = jnp.maximum(m_prev, jnp.max(sc, axis=1, keepdims=True))
            p = jnp.concatenate([jnp.exp2(sc[:, j * LANES:(j + 1) * LANES] - m_new)
                                 for j in range(tk // LANES)], axis=1)
            m_s[h] = m_new
            acc_s[h] = jnp.exp2(m_prev - m_new) * acc_s[h] + jnp.dot(
                p.astype(BF16), va_s[h, pl.ds(off, tk), :], preferred_element_type=F32)

    def two_full_steps(j, carry):
        step(2 * j, False)
        step(2 * j + 1, False)
        return carry

    lax.fori_loop(0, qi // 2, two_full_steps, 0)

    @pl.when(qi % 2 == 1)
    def _():
        step(qi - 1, False)

    step(qi, True)
    o0 = acc_s[0] / acc_s[0][:, HEAD_DIM:HEAD_DIM + 1]
    o1 = acc_s[1] / acc_s[1][:, HEAD_DIM:HEAD_DIM + 1]
    o_ref[...] = jnp.where(is_head, o0, pltpu.roll(o1, HEAD_DIM, 1)).astype(o_ref.dtype)


def _attention(q, k, v, ct, bsz, s):
    tq = min(ATT_TILE, s)
    nq = s // tq
    npair = N_HEADS // 2
    qspec = pl.BlockSpec((tq, LANES), lambda b, p, i: (b * nq + i, p))
    kvspec = pl.BlockSpec((s, LANES), lambda b, p, i: (b, p))
    return pl.pallas_call(
        _attn_body,
        grid=(bsz, npair, nq),
        in_specs=[qspec, kvspec, kvspec,
                  pl.BlockSpec((None, None, 2, tq), lambda b, p, i: (b, p, 0, i)),
                  pl.BlockSpec((None, None, 2, s), lambda b, p, i: (b, p, 0, 0))],
        out_specs=qspec,
        out_shape=jax.ShapeDtypeStruct((bsz * s, D_MODEL), BF16),
        scratch_shapes=[pltpu.VMEM((2, s, LANES), BF16), pltpu.VMEM((2, s, LANES), BF16),
                        pltpu.VMEM((2, tq, LANES), BF16), pltpu.VMEM((2, tq, LANES), F32),
                        pltpu.VMEM((2, tq, LANES), F32)],
        compiler_params=_params(("arbitrary", "arbitrary", "arbitrary")),
        name="attn",
    )(q, k, v, ct, ct)


def _merge_body(x_ref, ml_ref, yf_ref, gf_ref, wfo_ref, wo_ref, g_ref, wr_ref, br_ref,
                x2_ref, h2_ref, mi_ref, mw_ref, cnt_ref, running):
    tm = x_ref.shape[0]

    @pl.when(pl.program_id(0) == 0)
    def _():
        running[...] = jnp.zeros_like(running)

    fox = jnp.dot(yf_ref[...], wfo_ref[...], preferred_element_type=F32)
    merged = ml_ref[...].astype(F32) + jax.nn.sigmoid(gf_ref[...].astype(F32)) * fox
    x2 = x_ref[...] + jnp.dot(merged.astype(BF16), wo_ref[...], preferred_element_type=F32)
    x2_ref[...] = x2
    h2 = _rms(x2, g_ref[...])
    for c in range(ROW_SLABS):
        h2_ref[pl.ds(c, tm, stride=ROW_SLABS), :] = h2[:, c * LANES:(c + 1) * LANES]

    h_hi = h2.astype(BF16)
    h_lo = (h2 - h_hi.astype(F32)).astype(BF16)
    logits = (jnp.dot(h_hi, wr_ref[0], preferred_element_type=F32)
              + jnp.dot(h_lo, wr_ref[0], preferred_element_type=F32)
              + jnp.dot(h_hi, wr_ref[1], preferred_element_type=F32)) + br_ref[...]
    lane = lax.broadcasted_iota(I32, (tm, LANES), 1)
    lane_f = lane.astype(F32)
    logits = jnp.where(lane < N_EXPERTS, logits, -jnp.inf)
    vals, idxs = [], []
    for _ in range(TOP_K):
        mx = jnp.max(logits, axis=1, keepdims=True)
        ix = jnp.min(jnp.where(logits == mx, lane_f, float(LANES)), axis=1, keepdims=True).astype(I32)
        vals.append(mx)
        idxs.append(ix)
        logits = jnp.where(lane == ix, -jnp.inf, logits)
    exps = [jnp.exp(vk - vals[0]) for vk in vals]
    denom = exps[0] + exps[1] + exps[2] + exps[3]

    onehot = jnp.zeros((tm, LANES), F32)
    for ix in idxs:
        onehot = onehot + (lane == ix).astype(F32)
    r = lax.broadcasted_iota(I32, (tm, tm), 0)
    c = lax.broadcasted_iota(I32, (tm, tm), 1)
    before = jnp.dot((c < r).astype(BF16), onehot.astype(BF16), preferred_element_type=F32)
    pos = before + running[0:1, :]
    mi = jnp.zeros((tm, LANES), F32)
    mw = jnp.zeros((tm, LANES), F32)
    for kk in range(TOP_K):
        rank = jnp.sum(jnp.where(lane == idxs[kk], pos, 0.0), axis=1, keepdims=True)
        mi = jnp.where(lane == kk, idxs[kk].astype(F32), mi)
        mi = jnp.where(lane == TOP_K + kk, rank, mi)
        mw = jnp.where(lane == kk, exps[kk] / denom, mw)
    mi_ref[...] = mi.T[0:2 * TOP_K, :].astype(I32)
    mw_ref[...] = mw
    total = running[0:1, :] + jnp.sum(onehot, axis=0, keepdims=True)
    running[...] = jnp.broadcast_to(total, running.shape)
    cnt_ref[...] = jnp.broadcast_to(total, cnt_ref.shape).astype(I32)


def _merge(x2d, ml, yf, gf, w_fo, w_o, g_ffn, w_r, b_r):
    t = x2d.shape[0]
    tm = min(TOK_TILE, t)
    row = lambda i: (i, 0)
    tile = pl.BlockSpec((tm, D_MODEL), row)
    meta = pl.BlockSpec((tm, LANES), row)
    return pl.pallas_call(
        _merge_body,
        grid=(t // tm,),
        in_specs=[tile, tile, tile, tile, _const_spec(w_fo.shape), _const_spec(w_o.shape),
                  _const_spec((1, D_MODEL)), _const_spec(w_r.shape), _const_spec((1, LANES))],
        out_specs=[tile, pl.BlockSpec((tm * ROW_SLABS, LANES), row),
                   pl.BlockSpec((None, 2 * TOP_K, tm), lambda i: (i, 0, 0)), meta,
                   _const_spec((SUBLANES, LANES))],
        out_shape=[jax.ShapeDtypeStruct((t, D_MODEL), F32),
                   jax.ShapeDtypeStruct((t * ROW_SLABS, LANES), F32),
                   jax.ShapeDtypeStruct((t // tm, 2 * TOP_K, tm), I32),
                   jax.ShapeDtypeStruct((t, LANES), F32),
                   jax.ShapeDtypeStruct((SUBLANES, LANES), I32)],
        scratch_shapes=[pltpu.VMEM((SUBLANES, LANES), F32)],
        compiler_params=_params(("arbitrary",)),
        name="merge",
    )(x2d, ml, yf, gf, w_fo, w_o, g_ffn.reshape(1, D_MODEL), w_r, b_r)


def _row_slab(ref, row, base=0):
    return ref.at[pl.ds(pl.multiple_of(base + row * ROW_SLABS, ROW_SLABS), ROW_SLABS), :]


def _scatter_body(pad_start_ref, pad_ref, nu_ref, dest_ref, h2_ref, xs_hbm, zbuf, sem, zsem, *, n_tiles):
    tb = dest_ref.shape[-1] // TOP_K
    tile_rows = EXP_TILE * ROW_SLABS

    @pl.when(pl.program_id(0) == 0)
    def _():
        zbuf[...] = jnp.zeros_like(zbuf)

        def pad_pieces():
            for e in range(N_EXPERTS):
                pad = pad_ref[e]
                for bit in reversed(range(EXP_TILE.bit_length() - 1)):
                    size = 1 << bit
                    before = (pad >> (bit + 1)) << (bit + 1)
                    first = pl.multiple_of((pad_start_ref[e] + before) * ROW_SLABS, ROW_SLABS)
                    cp = pltpu.make_async_copy(zbuf.at[pl.ds(0, size * ROW_SLABS), :],
                                               xs_hbm.at[pl.ds(first, size * ROW_SLABS), :], zsem)
                    yield (pad & size) != 0, cp

        def tail_copy(j):
            return pltpu.make_async_copy(
                zbuf, xs_hbm.at[pl.ds(pl.multiple_of(j * tile_rows, tile_rows), tile_rows), :], zsem)

        for cond, cp in pad_pieces():
            pl.when(cond)(cp.start)
        lax.fori_loop(nu_ref[0], n_tiles, lambda j, c: (tail_copy(j).start(), c)[1], 0)
        for cond, cp in pad_pieces():
            pl.when(cond)(cp.wait)
        lax.fori_loop(nu_ref[0], n_tiles, lambda j, c: (tail_copy(j).wait(), c)[1], 0)

    def issue(j, carry):
        for kk in range(TOP_K):
            d = dest_ref[0, 0, kk * tb + j]
            pltpu.make_async_copy(_row_slab(h2_ref, j), _row_slab(xs_hbm, d), sem).start(
                priority=kk % 2)
        return carry

    lax.fori_loop(0, tb, issue, 0, unroll=ISSUE_UNROLL)
    n = tb * TOP_K * ROW_SLABS
    pltpu.make_async_copy(xs_hbm.at[pl.ds(0, n), :], xs_hbm.at[pl.ds(0, n), :], sem).wait()


def _scatter(dest3, pad_start, pad, n_used, h2s, n_tiles):
    nb, _, n = dest3.shape
    tb = n // TOP_K
    grid_spec = pltpu.PrefetchScalarGridSpec(
        num_scalar_prefetch=3,
        grid=(nb,),
        in_specs=[pl.BlockSpec((1, 1, n), lambda i, *_: (i, 0, 0), memory_space=pltpu.SMEM),
                  pl.BlockSpec((tb * ROW_SLABS, LANES), lambda i, *_: (i, 0))],
        out_specs=pl.BlockSpec(memory_space=pl.ANY),
        scratch_shapes=[pltpu.VMEM((EXP_TILE * ROW_SLABS, LANES), F32), pltpu.SemaphoreType.DMA(()),
                        pltpu.SemaphoreType.DMA(())],
    )
    return pl.pallas_call(
        functools.partial(_scatter_body, n_tiles=n_tiles),
        grid_spec=grid_spec,
        out_shape=jax.ShapeDtypeStruct((n_tiles * EXP_TILE * ROW_SLABS, LANES), F32),
        compiler_params=_params(("arbitrary",)),
        name="scatter",
    )(pad_start, pad, n_used, dest3, h2s)


def _expert_body(te_ref, first_ref, slot_ref, next_ref, nu_ref, xs_ref, wgu_hbm, bgu_ref, wd_hbm, bd_ref,
                 ys_ref, wgu_f, wd_f, wgu_bf, wd_bf, sems):
    i = pl.program_id(0)
    tm = xs_ref.shape[0] // ROW_SLABS

    def weight_copies(e, s):
        return (pltpu.make_async_copy(wgu_hbm.at[e], wgu_f.at[s], sems.at[0, s]),
                pltpu.make_async_copy(wd_hbm.at[e], wd_f.at[s], sems.at[1, s]))

    @pl.when(i == 0)
    def _():
        for cp in weight_copies(te_ref[0], 0):
            cp.start()

    @pl.when(first_ref[i] == 1)
    def _():
        s = slot_ref[i]
        for cp in weight_copies(te_ref[i], s):
            cp.wait()

        @pl.when(next_ref[i] >= 0)
        def _():
            for cp in weight_copies(next_ref[i], 1 - s):
                cp.start()

        wgu_bf[...] = wgu_f[s].astype(BF16)
        wd_bf[...] = wd_f[s].astype(BF16)

    @pl.when(i < nu_ref[0])
    def _():
        x = jnp.concatenate([xs_ref[pl.ds(c, tm, stride=ROW_SLABS), :] for c in range(ROW_SLABS)],
                            axis=1).astype(BF16)
        gu = jnp.dot(x, wgu_bf[...], preferred_element_type=F32) + bgu_ref[...]
        gate = jnp.minimum(gu[:, :D_FF], SWIGLU_LIMIT)
        up = jnp.clip(gu[:, D_FF:], -SWIGLU_LIMIT, SWIGLU_LIMIT)
        glu = gate * jax.nn.sigmoid(SWIGLU_ALPHA * gate)
        y = jnp.dot(((up + 1.0) * glu).astype(BF16), wd_bf[...],
                    preferred_element_type=F32) + bd_ref[...]
        for c in range(ROW_SLABS):
            ys_ref[pl.ds(c, tm, stride=ROW_SLABS), :] = y[:, c * LANES:(c + 1) * LANES]


def _experts(tile_expert, n_used, tile_end, xs, w_gate_up, b_gate_up, w_down, b_down, n_tiles):
    tm = EXP_TILE
    tile = jnp.arange(n_tiles, dtype=I32)
    used = tile < n_used[0]
    prev = jnp.concatenate([jnp.full((1,), -1, I32), tile_expert[:-1]])
    first = jnp.logical_and(used, tile_expert != prev).astype(I32)
    slot = ((jnp.cumsum(first) - 1) % 2).astype(I32)
    next_tile = tile_end[tile_expert].astype(I32)
    nxt = jnp.where(next_tile < n_used[0], tile_expert[jnp.minimum(next_tile, n_tiles - 1)], -1).astype(I32)

    rows = lambda i, te, fi, sl, nx, nu: (jnp.minimum(i, nu[0] - 1), 0)
    bsel = lambda i, te, fi, sl, nx, nu: (te[i], 0, 0)
    grid_spec = pltpu.PrefetchScalarGridSpec(
        num_scalar_prefetch=5,
        grid=(n_tiles,),
        in_specs=[pl.BlockSpec((tm * ROW_SLABS, LANES), rows),
                  pl.BlockSpec(memory_space=pl.ANY),
                  pl.BlockSpec((None, 1, 2 * D_FF), bsel),
                  pl.BlockSpec(memory_space=pl.ANY),
                  pl.BlockSpec((None, 1, D_MODEL), bsel)],
        out_specs=pl.BlockSpec((tm * ROW_SLABS, LANES), rows),
        scratch_shapes=[pltpu.VMEM((2, D_MODEL, 2 * D_FF), F32), pltpu.VMEM((2, D_FF, D_MODEL), F32),
                        pltpu.VMEM((D_MODEL, 2 * D_FF), BF16), pltpu.VMEM((D_FF, D_MODEL), BF16),
                        pltpu.SemaphoreType.DMA((2, 2))],
    )
    return pl.pallas_call(
        _expert_body,
        grid_spec=grid_spec,
        out_shape=jax.ShapeDtypeStruct(xs.shape, F32),
        input_output_aliases={5: 0},
        compiler_params=_params(("arbitrary",)),
        name="experts",
    )(tile_expert, first, slot, nxt, n_used, xs, w_gate_up, b_gate_up.reshape(N_EXPERTS, 1, 2 * D_FF),
      w_down, b_down.reshape(N_EXPERTS, 1, D_MODEL))


def _combine_body(dcur_ref, dnext_ref, x2_ref, mw_ref, g_ref, ys_hbm, o_ref, buf, sem):
    i = pl.program_id(0)
    tb = x2_ref.shape[0]
    slot_rows = TOP_K * tb * ROW_SLABS
    cur = i % 2

    def issue(d_ref, slot):
        base = pl.multiple_of(slot * slot_rows, slot_rows)

        def body(j, carry):
            for kk in range(TOP_K):
                d = d_ref[0, 0, kk * tb + j]
                pltpu.make_async_copy(_row_slab(ys_hbm, d), _row_slab(buf, kk * tb + j, base),
                                      sem.at[slot]).start(priority=kk % 2)
            return carry

        lax.fori_loop(0, tb, body, 0, unroll=ISSUE_UNROLL)

    @pl.when(i == 0)
    def _():
        issue(dcur_ref, 0)

    @pl.when(i + 1 < pl.num_programs(0))
    def _():
        issue(dnext_ref, 1 - cur)

    base = pl.multiple_of(cur * slot_rows, slot_rows)
    pltpu.make_async_copy(ys_hbm.at[pl.ds(0, slot_rows), :], buf.at[pl.ds(base, slot_rows), :],
                          sem.at[cur]).wait()

    mw = mw_ref[...]
    wk = [jnp.broadcast_to(mw[:, kk:kk + 1], (tb, LANES)) for kk in range(TOP_K)]
    cols = []
    for c in range(ROW_SLABS):
        acc = x2_ref[:, c * LANES:(c + 1) * LANES]
        for kk in range(TOP_K):
            rows = buf[pl.ds(base + kk * tb * ROW_SLABS + c, tb, stride=ROW_SLABS), :]
            acc = acc + wk[kk] * rows
        cols.append(acc)
    o_ref[...] = _rms(jnp.concatenate(cols, axis=1), g_ref[...])


def _combine(dest3, x2, mw, g_final, ys):
    t = x2.shape[0]
    nb, _, n = dest3.shape
    tb = n // TOP_K
    row = lambda i: (i, 0)
    dspec = lambda imap: pl.BlockSpec((1, 1, n), imap, memory_space=pltpu.SMEM)
    return pl.pallas_call(
        _combine_body,
        grid=(nb,),
        in_specs=[dspec(lambda i: (i, 0, 0)), dspec(lambda i: (jnp.minimum(i + 1, nb - 1), 0, 0)),
                  pl.BlockSpec((tb, D_MODEL), row), pl.BlockSpec((tb, LANES), row),
                  _const_spec((1, D_MODEL)), pl.BlockSpec(memory_space=pl.ANY)],
        out_specs=pl.BlockSpec((tb, D_MODEL), row),
        out_shape=jax.ShapeDtypeStruct((t, D_MODEL), F32),
        scratch_shapes=[pltpu.VMEM((2 * TOP_K * tb * ROW_SLABS, LANES), F32),
                        pltpu.SemaphoreType.DMA((2,))],
        compiler_params=_params(("arbitrary",)),
        name="combine",
    )(dest3, dest3, x2, mw, g_final.reshape(1, D_MODEL), ys)


def _block_diag_gates(w_rg, w_ig):
    per = GATE_CHUNK // RNN_BLOCK_DIM
    nchunk = w_rg.shape[0] // per
    eye = jnp.eye(per, dtype=w_rg.dtype)

    def bd(w):
        w = w.reshape(nchunk, per, RNN_BLOCK_DIM, RNN_BLOCK_DIM)
        return jnp.einsum('cpij,pq->cpiqj', w, eye).reshape(nchunk, GATE_CHUNK, GATE_CHUNK)

    return jnp.concatenate([bd(w_rg), bd(w_ig)], axis=-1).astype(BF16)


def kernel(x, g_mix, w_in, conv_w, conv_b, w_rg, b_rg, w_ig, b_ig, lru_lambda, b_f, w_lru_out,
           w_fox_out, w_o, g_ffn, w_router, b_router, w_gate_up, b_gate_up, w_down, b_down, g_final):
    bsz, s, d = x.shape
    assert d == D_MODEL
    t = bsz * s
    x2d = x.reshape(t, d)

    o_f = 5 * D_MODEL
    o_g = o_f + N_HEADS
    w_wide = jnp.concatenate([w_in[:, :o_f], w_in[:, o_g:]], axis=1).astype(BF16)
    w_f = jnp.pad(w_in[:, o_f:o_g], ((0, 0), (0, LANES - N_HEADS))).astype(BF16)
    b_f_pad = jnp.pad(b_f, (0, LANES - N_HEADS)).reshape(1, LANES)
    w_r = jnp.pad(w_router, ((0, 0), (0, LANES - N_EXPERTS)))
    w_r_hi = w_r.astype(BF16)
    w_r = jnp.stack([w_r_hi, (w_r - w_r_hi.astype(F32)).astype(BF16)])
    b_r = jnp.pad(b_router, (0, LANES - N_EXPERTS)).reshape(1, LANES)

    lx, lg, q, k, v, gl, gf, cum = _inproj(x2d, g_mix, w_wide, w_f, b_f_pad, s)

    ml = _lru(lx, lg, gl, conv_w, conv_b, _block_diag_gates(w_rg, w_ig), b_rg, b_ig, lru_lambda,
              w_lru_out.astype(BF16), bsz, s)

    ct = cum[:, :N_HEADS].reshape(bsz, s, N_HEADS // 2, 2).transpose(0, 2, 3, 1)
    yf = _attention(q, k, v, ct, bsz, s)

    x2, h2s, route, mw, cnt = _merge(x2d, ml, yf, gf, w_fox_out.astype(BF16), w_o.astype(BF16),
                                     g_ffn, w_r, b_r)

    assert TOK_TILE == ROW_TILE
    counts = cnt[0, :N_EXPERTS]
    tiles_e = (counts + EXP_TILE - 1) // EXP_TILE
    tile_end = jnp.cumsum(tiles_e)
    row_off = (tile_end - tiles_e) * EXP_TILE
    idx = route[:, :TOP_K, :]
    dest = route[:, TOP_K:, :]
    for e in range(N_EXPERTS):
        dest = dest + jnp.where(idx == e, row_off[e], 0)
    dest3 = dest.astype(I32).reshape(route.shape[0], 1, TOP_K * route.shape[2])
    n_tiles = (t * TOP_K) // EXP_TILE + N_EXPERTS
    n_used = tile_end[-1:].astype(I32)
    tile_ids = jnp.minimum(jnp.arange(n_tiles, dtype=I32), n_used[0] - 1)
    tile_expert = jnp.sum(tile_ids[:, None] >= tile_end[None, :], axis=1).astype(I32)

    xs = _scatter(dest3, (row_off + counts).astype(I32), (tiles_e * EXP_TILE - counts).astype(I32),
                  n_used, h2s, n_tiles)
    ys = _experts(tile_expert, n_used, tile_end, xs, w_gate_up, b_gate_up, w_down, b_down, n_tiles)
    out = _combine(dest3, x2, mw, g_final, ys)
    return out.reshape(bsz, s, d)
```

```python
import functools
import math

import jax
import jax.numpy as jnp
from jax import lax
from jax.experimental import pallas as pl
from jax.experimental.pallas import tpu as pltpu

F32 = jnp.float32
BF16 = jnp.bfloat16
I32 = jnp.int32

D_MODEL = 1024
RNN_BLOCK_DIM = 64
LRU_C = 8.0
HEAD_DIM = 64
N_HEADS = 16
N_EXPERTS = 32
TOP_K = 4
D_FF = 1024
SWIGLU_LIMIT = 7.0
SWIGLU_ALPHA = 1.702
RMS_EPS = 1e-6
LOG2E = 1.4426950408889634
Q_SCALE = LOG2E / math.sqrt(HEAD_DIM)
Q_GROUP = 2

LANES = 128
SUBLANES = 8
ROW_SLABS = D_MODEL // LANES
VMEM_LIMIT_BYTES = 56 * 1024 * 1024

TOK_TILE = 256
LRU_TILE = 256
ATT_TILE = 512
HEADS_PER_STEP = 4
EXP_TILE = 256
ROW_TILE = 256
ISSUE_UNROLL = 4
CONV_WIDTH = 4
GATE_CHUNK = 256


def _params(semantics):
    return pltpu.CompilerParams(dimension_semantics=semantics, vmem_limit_bytes=VMEM_LIMIT_BYTES)


def _rms(x, g):
    return x * lax.rsqrt(jnp.mean(x * x, axis=-1, keepdims=True) + RMS_EPS) * g


def _const_spec(shape):
    nd = len(shape)
    return pl.BlockSpec(shape, lambda *_: (0,) * nd)


def _inproj_body(x_ref, g_ref, w_ref, wf_ref, bf_ref, *rest, steps_per_seq):
    *wide, o_cum, carry = rest
    tm = x_ref.shape[0]

    @pl.when(pl.program_id(0) % steps_per_seq == 0)
    def _():
        carry[...] = jnp.zeros_like(carry)

    h = _rms(x_ref[...], g_ref[...]).astype(BF16)
    for j, o in enumerate(wide):
        acc = jnp.dot(h, w_ref[:, j * D_MODEL:(j + 1) * D_MODEL], preferred_element_type=F32)
        if j == Q_GROUP:
            acc = acc * Q_SCALE
        o[...] = acc.astype(o.dtype)

    z = jnp.dot(h, wf_ref[...], preferred_element_type=F32) + bf_ref[...]
    logf = jnp.minimum(z, 0.0) - jnp.log1p(jnp.exp(-jnp.abs(z)))
    r = lax.broadcasted_iota(I32, (tm, tm), 0)
    c = lax.broadcasted_iota(I32, (tm, tm), 1)
    tri = (r >= c).astype(BF16)
    cum = carry[0:1, :]
    for term in _split3(logf):
        cum = cum + jnp.dot(tri, term.astype(BF16), preferred_element_type=F32)
    o_cum[...] = cum
    carry[...] = jnp.broadcast_to(cum[tm - 1:tm, :], carry.shape)


def _inproj(x2d, g_mix, w_wide, w_f, b_f_pad, s):
    t = x2d.shape[0]
    n_wide = w_wide.shape[1] // D_MODEL
    tm = min(TOK_TILE, s)
    row = lambda i: (i, 0)
    return pl.pallas_call(
        functools.partial(_inproj_body, steps_per_seq=s // tm),
        grid=(t // tm,),
        in_specs=[pl.BlockSpec((tm, D_MODEL), row), _const_spec((1, D_MODEL)),
                  _const_spec(w_wide.shape), _const_spec(w_f.shape), _const_spec((1, LANES))],
        out_specs=[pl.BlockSpec((tm, D_MODEL), row)] * n_wide + [pl.BlockSpec((tm, LANES), row)],
        out_shape=[jax.ShapeDtypeStruct((t, D_MODEL), BF16)] * n_wide
                  + [jax.ShapeDtypeStruct((t, LANES), F32)],
        scratch_shapes=[pltpu.VMEM((SUBLANES, LANES), F32)],
        compiler_params=_params(("arbitrary",)),
        name="inproj",
    )(x2d, g_mix.reshape(1, D_MODEL), w_wide, w_f, b_f_pad)


def _lru_body(lx_ref, lg_ref, gl_ref, cw_ref, cb_ref, wbd_ref, brg_ref, big_ref, lam_ref, wo_ref,
              o_ref, xbuf, hcar, a_s, b_s, h_s):
    ts = lx_ref.shape[0]

    @pl.when(pl.program_id(1) == 0)
    def _():
        xbuf[0:SUBLANES, :] = jnp.zeros((SUBLANES, D_MODEL), F32)
        hcar[...] = jnp.zeros_like(hcar)

    x = lx_ref[...].astype(F32)
    xbuf[SUBLANES:SUBLANES + ts, :] = x
    xc = cb_ref[...] + cw_ref[CONV_WIDTH - 1:CONV_WIDTH, :] * x
    for k in range(CONV_WIDTH - 1):
        off = SUBLANES - (CONV_WIDTH - 1) + k
        xc = xc + cw_ref[k:k + 1, :] * xbuf[off:off + ts, :]
    xbuf[0:SUBLANES, :] = xbuf[ts:ts + SUBLANES, :]

    xcb = xc.astype(BF16)
    for c in range(D_MODEL // GATE_CHUNK):
        sl = slice(c * GATE_CHUNK, (c + 1) * GATE_CHUNK)
        g = jnp.dot(xcb[:, sl], wbd_ref[c], preferred_element_type=F32)
        rt = jax.nn.sigmoid(g[:, :GATE_CHUNK] + brg_ref[:, sl])
        it = jax.nn.sigmoid(g[:, GATE_CHUNK:] + big_ref[:, sl])
        nl = -lam_ref[:, sl]
        softplus = jnp.maximum(nl, 0.0) + jnp.log1p(jnp.exp(-jnp.abs(nl)))
        a = jnp.exp(-LRU_C * rt * softplus)
        a_s[:, sl] = a
        b_s[:, sl] = jnp.sqrt(1.0 - a * a) * (it * xc[:, sl])

    row = lax.broadcasted_iota(I32, (SUBLANES, D_MODEL), 0)

    def group(gidx, h0):
        off = pl.multiple_of(gidx * SUBLANES, SUBLANES)
        av = a_s[pl.ds(off, SUBLANES), :]
        bv = b_s[pl.ds(off, SUBLANES), :]
        for d in (1, 2, 4):
            keep = row >= d
            a_sh = jnp.where(keep, pltpu.roll(av, d, 0), 1.0)
            b_sh = jnp.where(keep, pltpu.roll(bv, d, 0), 0.0)
            bv = av * b_sh + bv
            av = av * a_sh
        hv = av * h0 + bv
        h_s[pl.ds(off, SUBLANES), :] = hv
        return jnp.broadcast_to(hv[SUBLANES - 1:SUBLANES, :], (SUBLANES, D_MODEL))

    hcar[...] = lax.fori_loop(0, ts // SUBLANES, group, hcar[...])

    y = (h_s[...] * jax.nn.gelu(lg_ref[...].astype(F32), approximate=True)).astype(BF16)
    proj = jnp.dot(y, wo_ref[...], preferred_element_type=F32)
    o_ref[...] = (jax.nn.sigmoid(gl_ref[...].astype(F32)) * proj).astype(o_ref.dtype)


def _lru(lx, lg, gl, conv_w, conv_b, wbd, b_rg, b_ig, lam, w_out, bsz, s):
    ts = min(LRU_TILE, s)
    ns = s // ts
    tile = pl.BlockSpec((ts, D_MODEL), lambda b, i: (b * ns + i, 0))
    vec = lambda a: a.reshape(1, D_MODEL)
    return pl.pallas_call(
        _lru_body,
        grid=(bsz, ns),
        in_specs=[tile, tile, tile, _const_spec((CONV_WIDTH, D_MODEL)), _const_spec((1, D_MODEL)),
                  _const_spec(wbd.shape), _const_spec((1, D_MODEL)), _const_spec((1, D_MODEL)),
                  _const_spec((1, D_MODEL)), _const_spec(w_out.shape)],
        out_specs=tile,
        out_shape=jax.ShapeDtypeStruct((bsz * s, D_MODEL), BF16),
        scratch_shapes=[pltpu.VMEM((ts + SUBLANES, D_MODEL), F32), pltpu.VMEM((SUBLANES, D_MODEL), F32),
                        pltpu.VMEM((ts, D_MODEL), F32), pltpu.VMEM((ts, D_MODEL), F32),
                        pltpu.VMEM((ts, D_MODEL), F32)],
        compiler_params=_params(("arbitrary", "arbitrary")),
        name="lru",
    )(lx, lg, gl, conv_w, vec(conv_b), wbd, vec(b_rg), vec(b_ig), vec(lam), w_out)


def _split3(c):
    hi = c.astype(BF16).astype(F32)
    rest = c - hi
    mid = rest.astype(BF16).astype(F32)
    return hi, mid, rest - mid


def _bias_columns(row_ref, h, start, n, query_side):
    hi, mid, lo = _split3(row_ref[h:h + 1, pl.ds(start, n)] * LOG2E)
    one = jnp.ones_like(hi)
    terms = (hi, mid, lo, one, one, one) if query_side else (one, one, one, -hi, -mid, -lo)
    sub = lax.broadcasted_iota(I32, (SUBLANES, n), 0)
    rows = jnp.zeros((SUBLANES, n), F32)
    for r, term in enumerate(terms):
        rows = jnp.where(sub == r, term, rows)
    above = jnp.zeros((HEAD_DIM, LANES), F32)
    below = jnp.zeros((LANES - HEAD_DIM - SUBLANES, LANES), F32)
    blocks = [jnp.concatenate([above, rows[:, j * LANES:(j + 1) * LANES], below], axis=0).T
              for j in range(n // LANES)]
    return jnp.concatenate(blocks, axis=0)


def _attn_body(q_ref, k_ref, v_ref, cq_ref, ck_ref, o_ref, ka_s, va_s, qa_s, m_s, acc_s):
    tq = q_ref.shape[0]
    tk = tq
    s_len = k_ref.shape[0]
    qi = pl.program_id(2)
    lane = lax.broadcasted_iota(I32, (1, LANES), 1)
    is_head = lane < HEAD_DIM
    ones_col = jnp.where(lane == HEAD_DIM, 1.0, 0.0)

    def head_lanes(x, h):
        g = x[:, (h // 2) * LANES:(h // 2 + 1) * LANES]
        return g if h % 2 == 0 else pltpu.roll(g, HEAD_DIM, 1)

    @pl.when(qi == 0)
    def _():
        def chunk(ci, carry):
            off = pl.multiple_of(ci * tk, tk)
            k2 = k_ref[pl.ds(off, tk), :].astype(F32)
            v2 = v_ref[pl.ds(off, tk), :].astype(F32)
            for h in range(HEADS_PER_STEP):
                ext = _bias_columns(ck_ref, h, off, tk, query_side=False)
                ka_s[h, pl.ds(off, tk), :] = jnp.where(is_head, head_lanes(k2, h), ext).astype(BF16)
                va_s[h, pl.ds(off, tk), :] = jnp.where(is_head, head_lanes(v2, h), ones_col).astype(BF16)
            return carry

        lax.fori_loop(0, s_len // tk, chunk, 0)

    q2 = q_ref[...].astype(F32)
    for h in range(HEADS_PER_STEP):
        ext = _bias_columns(cq_ref, h, 0, tq, query_side=True)
        qa_s[h] = jnp.where(is_head, head_lanes(q2, h), ext).astype(BF16)
    m_s[...] = jnp.full_like(m_s, -jnp.inf)
    acc_s[...] = jnp.zeros_like(acc_s)

    def step(ki, masked):
        off = pl.multiple_of(ki * tk, tk)
        for h in range(HEADS_PER_STEP):
            sc = lax.dot_general(qa_s[h], ka_s[h, pl.ds(off, tk), :], (((1,), (1,)), ((), ())),
                                 preferred_element_type=F32)
            if masked:
                r = lax.broadcasted_iota(I32, (tq, tk), 0)
                c = lax.broadcasted_iota(I32, (tq, tk), 1)
                sc = jnp.where(c <= r, sc, -jnp.inf)
            m_prev = m_s[h]
            m_new = jnp.maximum(m_prev, jnp.max(sc, axis=1, keepdims=True))
            p = jnp.concatenate([jnp.exp2(sc[:, j * LANES:(j + 1) * LANES] - m_new)
                                 for j in range(tk // LANES)], axis=1)
            m_s[h] = m_new
            acc_s[h] = jnp.exp2(m_prev - m_new) * acc_s[h] + jnp.dot(
                p.astype(BF16), va_s[h, pl.ds(off, tk), :], preferred_element_type=F32)

    def two_full_steps(j, carry):
        step(2 * j, False)
        step(2 * j + 1, False)
        return carry

    lax.fori_loop(0, qi // 2, two_full_steps, 0)

    @pl.when(qi % 2 == 1)
    def _():
        step(qi - 1, False)

    step(qi, True)
    groups = []
    for g in range(HEADS_PER_STEP // 2):
        o0 = acc_s[2 * g] / acc_s[2 * g][:, HEAD_DIM:HEAD_DIM + 1]
        o1 = acc_s[2 * g + 1] / acc_s[2 * g + 1][:, HEAD_DIM:HEAD_DIM + 1]
        groups.append(jnp.where(is_head, o0, pltpu.roll(o1, HEAD_DIM, 1)))
    o_ref[...] = jnp.concatenate(groups, axis=1).astype(o_ref.dtype)


def _attention(q, k, v, ct, bsz, s):
    tq = min(ATT_TILE, s)
    nq = s // tq
    hps = HEADS_PER_STEP
    width = hps * HEAD_DIM
    qspec = pl.BlockSpec((tq, width), lambda b, p, i: (b * nq + i, p))
    kvspec = pl.BlockSpec((s, width), lambda b, p, i: (b, p))
    return pl.pallas_call(
        _attn_body,
        grid=(bsz, N_HEADS // hps, nq),
        in_specs=[qspec, kvspec, kvspec,
                  pl.BlockSpec((None, None, hps, tq), lambda b, p, i: (b, p, 0, i)),
                  pl.BlockSpec((None, None, hps, s), lambda b, p, i: (b, p, 0, 0))],
        out_specs=qspec,
        out_shape=jax.ShapeDtypeStruct((bsz * s, D_MODEL), BF16),
        scratch_shapes=[pltpu.VMEM((hps, s, LANES), BF16), pltpu.VMEM((hps, s, LANES), BF16),
                        pltpu.VMEM((hps, tq, LANES), BF16), pltpu.VMEM((hps, tq, LANES), F32),
                        pltpu.VMEM((hps, tq, LANES), F32)],
        compiler_params=_params(("arbitrary", "arbitrary", "arbitrary")),
        name="attn",
    )(q, k, v, ct, ct)


def _merge_body(x_ref, ml_ref, yf_ref, gf_ref, wfo_ref, wo_ref, g_ref, wr_ref, br_ref,
                x2_ref, h2_ref, mi_ref, mw_ref, cnt_ref, running):
    tm = x_ref.shape[0]

    @pl.when(pl.program_id(0) == 0)
    def _():
        running[...] = jnp.zeros_like(running)

    fox = jnp.dot(yf_ref[...], wfo_ref[...], preferred_element_type=F32)
    merged = ml_ref[...].astype(F32) + jax.nn.sigmoid(gf_ref[...].astype(F32)) * fox
    x2 = x_ref[...] + jnp.dot(merged.astype(BF16), wo_ref[...], preferred_element_type=F32)
    x2_ref[...] = x2
    h2 = _rms(x2, g_ref[...])
    for c in range(ROW_SLABS):
        h2_ref[pl.ds(c, tm, stride=ROW_SLABS), :] = h2[:, c * LANES:(c + 1) * LANES]

    h_hi = h2.astype(BF16)
    h_lo = (h2 - h_hi.astype(F32)).astype(BF16)
    logits = (jnp.dot(h_hi, wr_ref[0], preferred_element_type=F32)
              + jnp.dot(h_lo, wr_ref[0], preferred_element_type=F32)
              + jnp.dot(h_hi, wr_ref[1], preferred_element_type=F32)) + br_ref[...]
    lane = lax.broadcasted_iota(I32, (tm, LANES), 1)
    lane_f = lane.astype(F32)
    logits = jnp.where(lane < N_EXPERTS, logits, -jnp.inf)
    vals, idxs = [], []
    for _ in range(TOP_K):
        mx = jnp.max(logits, axis=1, keepdims=True)
        ix = jnp.min(jnp.where(logits == mx, lane_f, float(LANES)), axis=1, keepdims=True).astype(I32)
        vals.append(mx)
        idxs.append(ix)
        logits = jnp.where(lane == ix, -jnp.inf, logits)
    exps = [jnp.exp(vk - vals[0]) for vk in vals]
    denom = exps[0] + exps[1] + exps[2] + exps[3]

    onehot = jnp.zeros((tm, LANES), F32)
    for ix in idxs:
        onehot = onehot + (lane == ix).astype(F32)
    r = lax.broadcasted_iota(I32, (tm, tm), 0)
    c = lax.broadcasted_iota(I32, (tm, tm), 1)
    before = jnp.dot((c < r).astype(BF16), onehot.astype(BF16), preferred_element_type=F32)
    pos = before + running[0:1, :]
    mi = jnp.zeros((tm, LANES), F32)
    mw = jnp.zeros((tm, LANES), F32)
    for kk in range(TOP_K):
        rank = jnp.sum(jnp.where(lane == idxs[kk], pos, 0.0), axis=1, keepdims=True)
        mi = jnp.where(lane == kk, idxs[kk].astype(F32), mi)
        mi = jnp.where(lane == TOP_K + kk, rank, mi)
        mw = jnp.where(lane == kk, exps[kk] / denom, mw)
    mi_ref[...] = mi.T[0:2 * TOP_K, :].astype(I32)
    mw_ref[...] = mw
    total = running[0:1, :] + jnp.sum(onehot, axis=0, keepdims=True)
    running[...] = jnp.broadcast_to(total, running.shape)
    cnt_ref[...] = jnp.broadcast_to(total, cnt_ref.shape).astype(I32)


def _merge(x2d, ml, yf, gf, w_fo, w_o, g_ffn, w_r, b_r):
    t = x2d.shape[0]
    tm = min(TOK_TILE, t)
    row = lambda i: (i, 0)
    tile = pl.BlockSpec((tm, D_MODEL), row)
    meta = pl.BlockSpec((tm, LANES), row)
    return pl.pallas_call(
        _merge_body,
        grid=(t // tm,),
        in_specs=[tile, tile, tile, tile, _const_spec(w_fo.shape), _const_spec(w_o.shape),
                  _const_spec((1, D_MODEL)), _const_spec(w_r.shape), _const_spec((1, LANES))],
        out_specs=[tile, pl.BlockSpec((tm * ROW_SLABS, LANES), row),
                   pl.BlockSpec((None, 2 * TOP_K, tm), lambda i: (i, 0, 0)), meta,
                   _const_spec((SUBLANES, LANES))],
        out_shape=[jax.ShapeDtypeStruct((t, D_MODEL), F32),
                   jax.ShapeDtypeStruct((t * ROW_SLABS, LANES), F32),
                   jax.ShapeDtypeStruct((t // tm, 2 * TOP_K, tm), I32),
                   jax.ShapeDtypeStruct((t, LANES), F32),
                   jax.ShapeDtypeStruct((SUBLANES, LANES), I32)],
        scratch_shapes=[pltpu.VMEM((SUBLANES, LANES), F32)],
        compiler_params=_params(("arbitrary",)),
        name="merge",
    )(x2d, ml, yf, gf, w_fo, w_o, g_ffn.reshape(1, D_MODEL), w_r, b_r)


def _row_slab(ref, row, base=0):
    return ref.at[pl.ds(pl.multiple_of(base + row * ROW_SLABS, ROW_SLABS), ROW_SLABS), :]


def _scatter_body(pad_start_ref, pad_ref, nu_ref, dest_ref, h2_ref, xs_hbm, zbuf, sem, zsem, *, n_tiles):
    tb = dest_ref.shape[-1] // TOP_K
    tile_rows = EXP_TILE * ROW_SLABS

    @pl.when(pl.program_id(0) == 0)
    def _():
        zbuf[...] = jnp.zeros_like(zbuf)

        def pad_pieces():
            for e in range(N_EXPERTS):
                pad = pad_ref[e]
                for bit in reversed(range(EXP_TILE.bit_length() - 1)):
                    size = 1 << bit
                    before = (pad >> (bit + 1)) << (bit + 1)
                    first = pl.multiple_of((pad_start_ref[e] + before) * ROW_SLABS, ROW_SLABS)
                    cp = pltpu.make_async_copy(zbuf.at[pl.ds(0, size * ROW_SLABS), :],
                                               xs_hbm.at[pl.ds(first, size * ROW_SLABS), :], zsem)
                    yield (pad & size) != 0, cp

        def tail_copy(j):
            return pltpu.make_async_copy(
                zbuf, xs_hbm.at[pl.ds(pl.multiple_of(j * tile_rows, tile_rows), tile_rows), :], zsem)

        for cond, cp in pad_pieces():
            pl.when(cond)(cp.start)
        lax.fori_loop(nu_ref[0], n_tiles, lambda j, c: (tail_copy(j).start(), c)[1], 0)
        for cond, cp in pad_pieces():
            pl.when(cond)(cp.wait)
        lax.fori_loop(nu_ref[0], n_tiles, lambda j, c: (tail_copy(j).wait(), c)[1], 0)

    def issue(j, carry):
        for kk in range(TOP_K):
            d = dest_ref[0, 0, kk * tb + j]
            pltpu.make_async_copy(_row_slab(h2_ref, j), _row_slab(xs_hbm, d), sem).start(
                priority=kk % 2)
        return carry

    lax.fori_loop(0, tb, issue, 0, unroll=ISSUE_UNROLL)
    n = tb * TOP_K * ROW_SLABS
    pltpu.make_async_copy(xs_hbm.at[pl.ds(0, n), :], xs_hbm.at[pl.ds(0, n), :], sem).wait()


def _scatter(dest3, pad_start, pad, n_used, h2s, n_tiles):
    nb, _, n = dest3.shape
    tb = n // TOP_K
    grid_spec = pltpu.PrefetchScalarGridSpec(
        num_scalar_prefetch=3,
        grid=(nb,),
        in_specs=[pl.BlockSpec((1, 1, n), lambda i, *_: (i, 0, 0), memory_space=pltpu.SMEM),
                  pl.BlockSpec((tb * ROW_SLABS, LANES), lambda i, *_: (i, 0))],
        out_specs=pl.BlockSpec(memory_space=pl.ANY),
        scratch_shapes=[pltpu.VMEM((EXP_TILE * ROW_SLABS, LANES), F32), pltpu.SemaphoreType.DMA(()),
                        pltpu.SemaphoreType.DMA(())],
    )
    return pl.pallas_call(
        functools.partial(_scatter_body, n_tiles=n_tiles),
        grid_spec=grid_spec,
        out_shape=jax.ShapeDtypeStruct((n_tiles * EXP_TILE * ROW_SLABS, LANES), F32),
        compiler_params=_params(("arbitrary",)),
        name="scatter",
    )(pad_start, pad, n_used, dest3, h2s)


def _expert_body(te_ref, first_ref, slot_ref, next_ref, nu_ref, xs_ref, wgu_hbm, bgu_ref, wd_hbm, bd_ref,
                 ys_ref, wgu_f, wd_f, wgu_bf, wd_bf, sems):
    i = pl.program_id(0)
    tm = xs_ref.shape[0] // ROW_SLABS

    def weight_copies(e, s):
        return (pltpu.make_async_copy(wgu_hbm.at[e], wgu_f.at[s], sems.at[0, s]),
                pltpu.make_async_copy(wd_hbm.at[e], wd_f.at[s], sems.at[1, s]))

    @pl.when(i == 0)
    def _():
        for cp in weight_copies(te_ref[0], 0):
            cp.start()

    @pl.when(first_ref[i] == 1)
    def _():
        s = slot_ref[i]
        for cp in weight_copies(te_ref[i], s):
            cp.wait()

        @pl.when(next_ref[i] >= 0)
        def _():
            for cp in weight_copies(next_ref[i], 1 - s):
                cp.start()

        wgu_bf[...] = wgu_f[s].astype(BF16)
        wd_bf[...] = wd_f[s].astype(BF16)

    @pl.when(i < nu_ref[0])
    def _():
        x = jnp.concatenate([xs_ref[pl.ds(c, tm, stride=ROW_SLABS), :] for c in range(ROW_SLABS)],
                            axis=1).astype(BF16)
        gu = jnp.dot(x, wgu_bf[...], preferred_element_type=F32) + bgu_ref[...]
        gate = jnp.minimum(gu[:, :D_FF], SWIGLU_LIMIT)
        up = jnp.clip(gu[:, D_FF:], -SWIGLU_LIMIT, SWIGLU_LIMIT)
        glu = gate * jax.nn.sigmoid(SWIGLU_ALPHA * gate)
        y = jnp.dot(((up + 1.0) * glu).astype(BF16), wd_bf[...],
                    preferred_element_type=F32) + bd_ref[...]
        for c in range(ROW_SLABS):
            ys_ref[pl.ds(c, tm, stride=ROW_SLABS), :] = y[:, c * LANES:(c + 1) * LANES]


def _experts(tile_expert, n_used, tile_end, xs, w_gate_up, b_gate_up, w_down, b_down, n_tiles):
    tm = EXP_TILE
    tile = jnp.arange(n_tiles, dtype=I32)
    used = tile < n_used[0]
    prev = jnp.concatenate([jnp.full((1,), -1, I32), tile_expert[:-1]])
    first = jnp.logical_and(used, tile_expert != prev).astype(I32)
    slot = ((jnp.cumsum(first) - 1) % 2).astype(I32)
    next_tile = tile_end[tile_expert].astype(I32)
    nxt = jnp.where(next_tile < n_used[0], tile_expert[jnp.minimum(next_tile, n_tiles - 1)], -1).astype(I32)

    rows = lambda i, te, fi, sl, nx, nu: (jnp.minimum(i, nu[0] - 1), 0)
    bsel = lambda i, te, fi, sl, nx, nu: (te[i], 0, 0)
    grid_spec = pltpu.PrefetchScalarGridSpec(
        num_scalar_prefetch=5,
        grid=(n_tiles,),
        in_specs=[pl.BlockSpec((tm * ROW_SLABS, LANES), rows),
                  pl.BlockSpec(memory_space=pl.ANY),
                  pl.BlockSpec((None, 1, 2 * D_FF), bsel),
                  pl.BlockSpec(memory_space=pl.ANY),
                  pl.BlockSpec((None, 1, D_MODEL), bsel)],
        out_specs=pl.BlockSpec((tm * ROW_SLABS, LANES), rows),
        scratch_shapes=[pltpu.VMEM((2, D_MODEL, 2 * D_FF), F32), pltpu.VMEM((2, D_FF, D_MODEL), F32),
                        pltpu.VMEM((D_MODEL, 2 * D_FF), BF16), pltpu.VMEM((D_FF, D_MODEL), BF16),
                        pltpu.SemaphoreType.DMA((2, 2))],
    )
    return pl.pallas_call(
        _expert_body,
        grid_spec=grid_spec,
        out_shape=jax.ShapeDtypeStruct(xs.shape, F32),
        input_output_aliases={5: 0},
        compiler_params=_params(("arbitrary",)),
        name="experts",
    )(tile_expert, first, slot, nxt, n_used, xs, w_gate_up, b_gate_up.reshape(N_EXPERTS, 1, 2 * D_FF),
      w_down, b_down.reshape(N_EXPERTS, 1, D_MODEL))


def _combine_body(dcur_ref, dnext_ref, x2_ref, mw_ref, g_ref, ys_hbm, o_ref, buf, sem):
    i = pl.program_id(0)
    tb = x2_ref.shape[0]
    slot_rows = TOP_K * tb * ROW_SLABS
    cur = i % 2

    def issue(d_ref, slot):
        base = pl.multiple_of(slot * slot_rows, slot_rows)

        def body(j, carry):
            for kk in range(TOP_K):
                d = d_ref[0, 0, kk * tb + j]
                pltpu.make_async_copy(_row_slab(ys_hbm, d), _row_slab(buf, kk * tb + j, base),
                                      sem.at[slot]).start(priority=kk % 2)
            return carry

        lax.fori_loop(0, tb, body, 0, unroll=ISSUE_UNROLL)

    @pl.when(i == 0)
    def _():
        issue(dcur_ref, 0)

    @pl.when(i + 1 < pl.num_programs(0))
    def _():
        issue(dnext_ref, 1 - cur)

    base = pl.multiple_of(cur * slot_rows, slot_rows)
    pltpu.make_async_copy(ys_hbm.at[pl.ds(0, slot_rows), :], buf.at[pl.ds(base, slot_rows), :],
                          sem.at[cur]).wait()

    mw = mw_ref[...]
    wk = [jnp.broadcast_to(mw[:, kk:kk + 1], (tb, LANES)) for kk in range(TOP_K)]
    cols = []
    for c in range(ROW_SLABS):
        acc = x2_ref[:, c * LANES:(c + 1) * LANES]
        for kk in range(TOP_K):
            rows = buf[pl.ds(base + kk * tb * ROW_SLABS + c, tb, stride=ROW_SLABS), :]
            acc = acc + wk[kk] * rows
        cols.append(acc)
    o_ref[...] = _rms(jnp.concatenate(cols, axis=1), g_ref[...])


def _combine(dest3, x2, mw, g_final, ys):
    t = x2.shape[0]
    nb, _, n = dest3.shape
    tb = n // TOP_K
    row = lambda i: (i, 0)
    dspec = lambda imap: pl.BlockSpec((1, 1, n), imap, memory_space=pltpu.SMEM)
    return pl.pallas_call(
        _combine_body,
        grid=(nb,),
        in_specs=[dspec(lambda i: (i, 0, 0)), dspec(lambda i: (jnp.minimum(i + 1, nb - 1), 0, 0)),
                  pl.BlockSpec((tb, D_MODEL), row), pl.BlockSpec((tb, LANES), row),
                  _const_spec((1, D_MODEL)), pl.BlockSpec(memory_space=pl.ANY)],
        out_specs=pl.BlockSpec((tb, D_MODEL), row),
        out_shape=jax.ShapeDtypeStruct((t, D_MODEL), F32),
        scratch_shapes=[pltpu.VMEM((2 * TOP_K * tb * ROW_SLABS, LANES), F32),
                        pltpu.SemaphoreType.DMA((2,))],
        compiler_params=_params(("arbitrary",)),
        name="combine",
    )(dest3, dest3, x2, mw, g_final.reshape(1, D_MODEL), ys)


def _block_diag_gates(w_rg, w_ig):
    per = GATE_CHUNK // RNN_BLOCK_DIM
    nchunk = w_rg.shape[0] // per
    eye = jnp.eye(per, dtype=w_rg.dtype)

    def bd(w):
        w = w.reshape(nchunk, per, RNN_BLOCK_DIM, RNN_BLOCK_DIM)
        return jnp.einsum('cpij,pq->cpiqj', w, eye).reshape(nchunk, GATE_CHUNK, GATE_CHUNK)

    return jnp.concatenate([bd(w_rg), bd(w_ig)], axis=-1).astype(BF16)


def kernel(x, g_mix, w_in, conv_w, conv_b, w_rg, b_rg, w_ig, b_ig, lru_lambda, b_f, w_lru_out,
           w_fox_out, w_o, g_ffn, w_router, b_router, w_gate_up, b_gate_up, w_down, b_down, g_final):
    bsz, s, d = x.shape
    assert d == D_MODEL
    t = bsz * s
    x2d = x.reshape(t, d)

    o_f = 5 * D_MODEL
    o_g = o_f + N_HEADS
    w_wide = jnp.concatenate([w_in[:, :o_f], w_in[:, o_g:]], axis=1).astype(BF16)
    w_f = jnp.pad(w_in[:, o_f:o_g], ((0, 0), (0, LANES - N_HEADS))).astype(BF16)
    b_f_pad = jnp.pad(b_f, (0, LANES - N_HEADS)).reshape(1, LANES)
    w_r = jnp.pad(w_router, ((0, 0), (0, LANES - N_EXPERTS)))
    w_r_hi = w_r.astype(BF16)
    w_r = jnp.stack([w_r_hi, (w_r - w_r_hi.astype(F32)).astype(BF16)])
    b_r = jnp.pad(b_router, (0, LANES - N_EXPERTS)).reshape(1, LANES)

    lx, lg, q, k, v, gl, gf, cum = _inproj(x2d, g_mix, w_wide, w_f, b_f_pad, s)

    ml = _lru(lx, lg, gl, conv_w, conv_b, _block_diag_gates(w_rg, w_ig), b_rg, b_ig, lru_lambda,
              w_lru_out.astype(BF16), bsz, s)

    ct = cum[:, :N_HEADS].reshape(bsz, s, N_HEADS // HEADS_PER_STEP, HEADS_PER_STEP).transpose(0, 2, 3, 1)
    yf = _attention(q, k, v, ct, bsz, s)

    x2, h2s, route, mw, cnt = _merge(x2d, ml, yf, gf, w_fox_out.astype(BF16), w_o.astype(BF16),
                                     g_ffn, w_r, b_r)

    assert TOK_TILE == ROW_TILE
    counts = cnt[0, :N_EXPERTS]
    tiles_e = (counts + EXP_TILE - 1) // EXP_TILE
    tile_end = jnp.cumsum(tiles_e)
    row_off = (tile_end - tiles_e) * EXP_TILE
    idx = route[:, :TOP_K, :]
    dest = route[:, TOP_K:, :]
    for e in range(N_EXPERTS):
        dest = dest + jnp.where(idx == e, row_off[e], 0)
    dest3 = dest.astype(I32).reshape(route.shape[0], 1, TOP_K * route.shape[2])
    n_tiles = (t * TOP_K) // EXP_TILE + N_EXPERTS
    n_used = tile_end[-1:].astype(I32)
    tile_ids = jnp.minimum(jnp.arange(n_tiles, dtype=I32), n_used[0] - 1)
    tile_expert = jnp.sum(tile_ids[:, None] >= tile_end[None, :], axis=1).astype(I32)

    xs = _scatter(dest3, (row_off + counts).astype(I32), (tiles_e * EXP_TILE - counts).astype(I32),
                  n_used, h2s, n_tiles)
    ys = _experts(tile_expert, n_used, tile_end, xs, w_gate_up, b_gate_up, w_down, b_down, n_tiles)
    out = _combine(dest3, x2, mw, g_final, ys)
    return out.reshape(bsz, s, d)
```

```python
import functools
import math

import jax
import jax.numpy as jnp
from jax import lax
from jax.experimental import pallas as pl
from jax.experimental.pallas import tpu as pltpu

F32 = jnp.float32
BF16 = jnp.bfloat16
I32 = jnp.int32

D_MODEL = 1024
RNN_BLOCK_DIM = 64
LRU_C = 8.0
HEAD_DIM = 64
N_HEADS = 16
N_EXPERTS = 32
TOP_K = 4
D_FF = 1024
SWIGLU_LIMIT = 7.0
SWIGLU_ALPHA = 1.702
RMS_EPS = 1e-6
LOG2E = 1.4426950408889634
Q_SCALE = LOG2E / math.sqrt(HEAD_DIM)
Q_GROUP = 2

LANES = 128
SUBLANES = 8
ROW_SLABS = D_MODEL // LANES
VMEM_LIMIT_BYTES = 56 * 1024 * 1024

TOK_TILE = 256
LRU_TILE = 256
ATT_TILE = 512
HEADS_PER_STEP = 4
EXP_TILE = 256
ROW_TILE = 256
ISSUE_UNROLL = 4
CONV_WIDTH = 4
GATE_CHUNK = 256


def _params(semantics):
    return pltpu.CompilerParams(dimension_semantics=semantics, vmem_limit_bytes=VMEM_LIMIT_BYTES)


def _rms(x, g):
    return x * lax.rsqrt(jnp.mean(x * x, axis=-1, keepdims=True) + RMS_EPS) * g


def _const_spec(shape):
    nd = len(shape)
    return pl.BlockSpec(shape, lambda *_: (0,) * nd)


def _inproj_body(x_ref, g_ref, w_ref, wf_ref, bf_ref, *rest, steps_per_seq):
    *wide, o_cum, carry = rest
    tm = x_ref.shape[0]

    @pl.when(pl.program_id(0) % steps_per_seq == 0)
    def _():
        carry[...] = jnp.zeros_like(carry)

    h = _rms(x_ref[...], g_ref[...]).astype(BF16)
    for j, o in enumerate(wide):
        acc = jnp.dot(h, w_ref[:, j * D_MODEL:(j + 1) * D_MODEL], preferred_element_type=F32)
        if j == Q_GROUP:
            acc = acc * Q_SCALE
        o[...] = acc.astype(o.dtype)

    z = jnp.dot(h, wf_ref[...], preferred_element_type=F32) + bf_ref[...]
    logf = jnp.minimum(z, 0.0) - jnp.log1p(jnp.exp(-jnp.abs(z)))
    r = lax.broadcasted_iota(I32, (tm, tm), 0)
    c = lax.broadcasted_iota(I32, (tm, tm), 1)
    tri = (r >= c).astype(BF16)
    cum = carry[0:1, :]
    for term in _split3(logf):
        cum = cum + jnp.dot(tri, term.astype(BF16), preferred_element_type=F32)
    o_cum[...] = cum
    carry[...] = jnp.broadcast_to(cum[tm - 1:tm, :], carry.shape)


def _inproj(x2d, g_mix, w_wide, w_f, b_f_pad, s):
    t = x2d.shape[0]
    n_wide = w_wide.shape[1] // D_MODEL
    tm = min(TOK_TILE, s)
    row = lambda i: (i, 0)
    return pl.pallas_call(
        functools.partial(_inproj_body, steps_per_seq=s // tm),
        grid=(t // tm,),
        in_specs=[pl.BlockSpec((tm, D_MODEL), row), _const_spec((1, D_MODEL)),
                  _const_spec(w_wide.shape), _const_spec(w_f.shape), _const_spec((1, LANES))],
        out_specs=[pl.BlockSpec((tm, D_MODEL), row)] * n_wide + [pl.BlockSpec((tm, LANES), row)],
        out_shape=[jax.ShapeDtypeStruct((t, D_MODEL), BF16)] * n_wide
                  + [jax.ShapeDtypeStruct((t, LANES), F32)],
        scratch_shapes=[pltpu.VMEM((SUBLANES, LANES), F32)],
        compiler_params=_params(("arbitrary",)),
        name="inproj",
    )(x2d, g_mix.reshape(1, D_MODEL), w_wide, w_f, b_f_pad)


def _lru_body(lx_ref, lg_ref, gl_ref, cw_ref, cb_ref, wbd_ref, brg_ref, big_ref, lam_ref, wo_ref,
              o_ref, xbuf, hcar, a_s, b_s, h_s):
    ts = lx_ref.shape[0]

    @pl.when(pl.program_id(1) == 0)
    def _():
        xbuf[0:SUBLANES, :] = jnp.zeros((SUBLANES, D_MODEL), F32)
        hcar[...] = jnp.zeros_like(hcar)

    x = lx_ref[...].astype(F32)
    xbuf[SUBLANES:SUBLANES + ts, :] = x
    xc = cb_ref[...] + cw_ref[CONV_WIDTH - 1:CONV_WIDTH, :] * x
    for k in range(CONV_WIDTH - 1):
        off = SUBLANES - (CONV_WIDTH - 1) + k
        xc = xc + cw_ref[k:k + 1, :] * xbuf[off:off + ts, :]
    xbuf[0:SUBLANES, :] = xbuf[ts:ts + SUBLANES, :]

    xcb = xc.astype(BF16)
    for c in range(D_MODEL // GATE_CHUNK):
        sl = slice(c * GATE_CHUNK, (c + 1) * GATE_CHUNK)
        g = jnp.dot(xcb[:, sl], wbd_ref[c], preferred_element_type=F32)
        rt = jax.nn.sigmoid(g[:, :GATE_CHUNK] + brg_ref[:, sl])
        it = jax.nn.sigmoid(g[:, GATE_CHUNK:] + big_ref[:, sl])
        nl = -lam_ref[:, sl]
        softplus = jnp.maximum(nl, 0.0) + jnp.log1p(jnp.exp(-jnp.abs(nl)))
        a = jnp.exp(-LRU_C * rt * softplus)
        a_s[:, sl] = a
        b_s[:, sl] = jnp.sqrt(1.0 - a * a) * (it * xc[:, sl])

    row = lax.broadcasted_iota(I32, (SUBLANES, D_MODEL), 0)

    def group(gidx, h0):
        off = pl.multiple_of(gidx * SUBLANES, SUBLANES)
        av = a_s[pl.ds(off, SUBLANES), :]
        bv = b_s[pl.ds(off, SUBLANES), :]
        for d in (1, 2, 4):
            keep = row >= d
            a_sh = jnp.where(keep, pltpu.roll(av, d, 0), 1.0)
            b_sh = jnp.where(keep, pltpu.roll(bv, d, 0), 0.0)
            bv = av * b_sh + bv
            av = av * a_sh
        hv = av * h0 + bv
        h_s[pl.ds(off, SUBLANES), :] = hv
        return jnp.broadcast_to(hv[SUBLANES - 1:SUBLANES, :], (SUBLANES, D_MODEL))

    hcar[...] = lax.fori_loop(0, ts // SUBLANES, group, hcar[...])

    y = (h_s[...] * jax.nn.gelu(lg_ref[...].astype(F32), approximate=True)).astype(BF16)
    proj = jnp.dot(y, wo_ref[...], preferred_element_type=F32)
    o_ref[...] = (jax.nn.sigmoid(gl_ref[...].astype(F32)) * proj).astype(o_ref.dtype)


def _lru(lx, lg, gl, conv_w, conv_b, wbd, b_rg, b_ig, lam, w_out, bsz, s):
    ts = min(LRU_TILE, s)
    ns = s // ts
    tile = pl.BlockSpec((ts, D_MODEL), lambda b, i: (b * ns + i, 0))
    vec = lambda a: a.reshape(1, D_MODEL)
    return pl.pallas_call(
        _lru_body,
        grid=(bsz, ns),
        in_specs=[tile, tile, tile, _const_spec((CONV_WIDTH, D_MODEL)), _const_spec((1, D_MODEL)),
                  _const_spec(wbd.shape), _const_spec((1, D_MODEL)), _const_spec((1, D_MODEL)),
                  _const_spec((1, D_MODEL)), _const_spec(w_out.shape)],
        out_specs=tile,
        out_shape=jax.ShapeDtypeStruct((bsz * s, D_MODEL), BF16),
        scratch_shapes=[pltpu.VMEM((ts + SUBLANES, D_MODEL), F32), pltpu.VMEM((SUBLANES, D_MODEL), F32),
                        pltpu.VMEM((ts, D_MODEL), F32), pltpu.VMEM((ts, D_MODEL), F32),
                        pltpu.VMEM((ts, D_MODEL), F32)],
        compiler_params=_params(("arbitrary", "arbitrary")),
        name="lru",
    )(lx, lg, gl, conv_w, vec(conv_b), wbd, vec(b_rg), vec(b_ig), vec(lam), w_out)


def _split3(c):
    hi = c.astype(BF16).astype(F32)
    rest = c - hi
    mid = rest.astype(BF16).astype(F32)
    return hi, mid, rest - mid


def _bias_columns(row_ref, h, start, n, query_side):
    hi, mid, lo = _split3(row_ref[h:h + 1, pl.ds(start, n)] * LOG2E)
    one = jnp.ones_like(hi)
    terms = (hi, mid, lo, one, one, one) if query_side else (one, one, one, -hi, -mid, -lo)
    sub = lax.broadcasted_iota(I32, (SUBLANES, n), 0)
    rows = jnp.zeros((SUBLANES, n), F32)
    for r, term in enumerate(terms):
        rows = jnp.where(sub == r, term, rows)
    above = jnp.zeros((HEAD_DIM, LANES), F32)
    below = jnp.zeros((LANES - HEAD_DIM - SUBLANES, LANES), F32)
    blocks = [jnp.concatenate([above, rows[:, j * LANES:(j + 1) * LANES], below], axis=0).T
              for j in range(n // LANES)]
    return jnp.concatenate(blocks, axis=0)


def _attn_body(q_ref, k_ref, v_ref, cq_ref, ck_ref, o_ref, ka_s, va_s, qa_s, m_s, acc_s):
    tq = q_ref.shape[0]
    tk = tq
    s_len = k_ref.shape[0]
    qi = pl.program_id(2)
    lane = lax.broadcasted_iota(I32, (1, LANES), 1)
    is_head = lane < HEAD_DIM
    ones_col = jnp.where(lane == HEAD_DIM, 1.0, 0.0)

    def head_lanes(x, h):
        g = x[:, (h // 2) * LANES:(h // 2 + 1) * LANES]
        return g if h % 2 == 0 else pltpu.roll(g, HEAD_DIM, 1)

    @pl.when(qi == 0)
    def _():
        def chunk(ci, carry):
            off = pl.multiple_of(ci * tk, tk)
            k2 = k_ref[pl.ds(off, tk), :].astype(F32)
            v2 = v_ref[pl.ds(off, tk), :].astype(F32)
            for h in range(HEADS_PER_STEP):
                ext = _bias_columns(ck_ref, h, off, tk, query_side=False)
                ka_s[h, pl.ds(off, tk), :] = jnp.where(is_head, head_lanes(k2, h), ext).astype(BF16)
                va_s[h, pl.ds(off, tk), :] = jnp.where(is_head, head_lanes(v2, h), ones_col).astype(BF16)
            return carry

        lax.fori_loop(0, s_len // tk, chunk, 0)

    q2 = q_ref[...].astype(F32)
    for h in range(HEADS_PER_STEP):
        ext = _bias_columns(cq_ref, h, 0, tq, query_side=True)
        qa_s[h] = jnp.where(is_head, head_lanes(q2, h), ext).astype(BF16)
    m_s[...] = jnp.full_like(m_s, -jnp.inf)
    acc_s[...] = jnp.zeros_like(acc_s)

    def step(first_key, nk, diag_shift):
        off = pl.multiple_of(first_key, tk)
        for h in range(HEADS_PER_STEP):
            sc = lax.dot_general(qa_s[h], ka_s[h, pl.ds(off, nk), :], (((1,), (1,)), ((), ())),
                                 preferred_element_type=F32)
            if diag_shift is not None:
                r = lax.broadcasted_iota(I32, (tq, nk), 0)
                c = lax.broadcasted_iota(I32, (tq, nk), 1)
                sc = jnp.where(c <= r + diag_shift, sc, -jnp.inf)
            m_prev = m_s[h]
            m_new = jnp.maximum(m_prev, jnp.max(sc, axis=1, keepdims=True))
            p = jnp.concatenate([jnp.exp2(sc[:, j * LANES:(j + 1) * LANES] - m_new)
                                 for j in range(nk // LANES)], axis=1)
            m_s[h] = m_new
            acc_s[h] = jnp.exp2(m_prev - m_new) * acc_s[h] + jnp.dot(
                p.astype(BF16), va_s[h, pl.ds(off, nk), :], preferred_element_type=F32)

    def wide_step(j, carry):
        step(j * (2 * tk), 2 * tk, None)
        return carry

    lax.fori_loop(0, qi // 2, wide_step, 0)

    @pl.when(qi % 2 == 1)
    def _():
        step((qi - 1) * tk, 2 * tk, tk)

    @pl.when(qi % 2 == 0)
    def _():
        step(qi * tk, tk, 0)

    groups = []
    for g in range(HEADS_PER_STEP // 2):
        o0 = acc_s[2 * g] / acc_s[2 * g][:, HEAD_DIM:HEAD_DIM + 1]
        o1 = acc_s[2 * g + 1] / acc_s[2 * g + 1][:, HEAD_DIM:HEAD_DIM + 1]
        groups.append(jnp.where(is_head, o0, pltpu.roll(o1, HEAD_DIM, 1)))
    o_ref[...] = jnp.concatenate(groups, axis=1).astype(o_ref.dtype)


def _attention(q, k, v, ct, bsz, s):
    tq = min(ATT_TILE, s)
    nq = s // tq
    hps = HEADS_PER_STEP
    width = hps * HEAD_DIM
    qspec = pl.BlockSpec((tq, width), lambda b, p, i: (b * nq + i, p))
    kvspec = pl.BlockSpec((s, width), lambda b, p, i: (b, p))
    return pl.pallas_call(
        _attn_body,
        grid=(bsz, N_HEADS // hps, nq),
        in_specs=[qspec, kvspec, kvspec,
                  pl.BlockSpec((None, None, hps, tq), lambda b, p, i: (b, p, 0, i)),
                  pl.BlockSpec((None, None, hps, s), lambda b, p, i: (b, p, 0, 0))],
        out_specs=qspec,
        out_shape=jax.ShapeDtypeStruct((bsz * s, D_MODEL), BF16),
        scratch_shapes=[pltpu.VMEM((hps, s, LANES), BF16), pltpu.VMEM((hps, s, LANES), BF16),
                        pltpu.VMEM((hps, tq, LANES), BF16), pltpu.VMEM((hps, tq, LANES), F32),
                        pltpu.VMEM((hps, tq, LANES), F32)],
        compiler_params=_params(("arbitrary", "arbitrary", "arbitrary")),
        name="attn",
    )(q, k, v, ct, ct)


def _merge_body(x_ref, ml_ref, yf_ref, gf_ref, wfo_ref, wo_ref, g_ref, wr_ref, br_ref,
                x2_ref, h2_ref, mi_ref, mw_ref, cnt_ref, running):
    tm = x_ref.shape[0]

    @pl.when(pl.program_id(0) == 0)
    def _():
        running[...] = jnp.zeros_like(running)

    fox = jnp.dot(yf_ref[...], wfo_ref[...], preferred_element_type=F32)
    merged = ml_ref[...].astype(F32) + jax.nn.sigmoid(gf_ref[...].astype(F32)) * fox
    x2 = x_ref[...] + jnp.dot(merged.astype(BF16), wo_ref[...], preferred_element_type=F32)
    x2_ref[...] = x2
    h2 = _rms(x2, g_ref[...])
    for c in range(ROW_SLABS):
        h2_ref[pl.ds(c, tm, stride=ROW_SLABS), :] = h2[:, c * LANES:(c + 1) * LANES]

    h_hi = h2.astype(BF16)
    h_lo = (h2 - h_hi.astype(F32)).astype(BF16)
    logits = (jnp.dot(h_hi, wr_ref[0], preferred_element_type=F32)
              + jnp.dot(h_lo, wr_ref[0], preferred_element_type=F32)
              + jnp.dot(h_hi, wr_ref[1], preferred_element_type=F32)) + br_ref[...]
    lane = lax.broadcasted_iota(I32, (tm, LANES), 1)
    lane_f = lane.astype(F32)
    logits = jnp.where(lane < N_EXPERTS, logits, -jnp.inf)
    vals, idxs = [], []
    for _ in range(TOP_K):
        mx = jnp.max(logits, axis=1, keepdims=True)
        ix = jnp.min(jnp.where(logits == mx, lane_f, float(LANES)), axis=1, keepdims=True).astype(I32)
        vals.append(mx)
        idxs.append(ix)
        logits = jnp.where(lane == ix, -jnp.inf, logits)
    exps = [jnp.exp(vk - vals[0]) for vk in vals]
    denom = exps[0] + exps[1] + exps[2] + exps[3]

    onehot = jnp.zeros((tm, LANES), F32)
    for ix in idxs:
        onehot = onehot + (lane == ix).astype(F32)
    r = lax.broadcasted_iota(I32, (tm, tm), 0)
    c = lax.broadcasted_iota(I32, (tm, tm), 1)
    before = jnp.dot((c < r).astype(BF16), onehot.astype(BF16), preferred_element_type=F32)
    pos = before + running[0:1, :]
    mi = jnp.zeros((tm, LANES), F32)
    mw = jnp.zeros((tm, LANES), F32)
    for kk in range(TOP_K):
        rank = jnp.sum(jnp.where(lane == idxs[kk], pos, 0.0), axis=1, keepdims=True)
        mi = jnp.where(lane == kk, idxs[kk].astype(F32), mi)
        mi = jnp.where(lane == TOP_K + kk, rank, mi)
        mw = jnp.where(lane == kk, exps[kk] / denom, mw)
    mi_ref[...] = mi.T[0:2 * TOP_K, :].astype(I32)
    mw_ref[...] = mw
    total = running[0:1, :] + jnp.sum(onehot, axis=0, keepdims=True)
    running[...] = jnp.broadcast_to(total, running.shape)
    cnt_ref[...] = jnp.broadcast_to(total, cnt_ref.shape).astype(I32)


def _merge(x2d, ml, yf, gf, w_fo, w_o, g_ffn, w_r, b_r):
    t = x2d.shape[0]
    tm = min(TOK_TILE, t)
    row = lambda i: (i, 0)
    tile = pl.BlockSpec((tm, D_MODEL), row)
    meta = pl.BlockSpec((tm, LANES), row)
    return pl.pallas_call(
        _merge_body,
        grid=(t // tm,),
        in_specs=[tile, tile, tile, tile, _const_spec(w_fo.shape), _const_spec(w_o.shape),
                  _const_spec((1, D_MODEL)), _const_spec(w_r.shape), _const_spec((1, LANES))],
        out_specs=[tile, pl.BlockSpec((tm * ROW_SLABS, LANES), row),
                   pl.BlockSpec((None, 2 * TOP_K, tm), lambda i: (i, 0, 0)), meta,
                   _const_spec((SUBLANES, LANES))],
        out_shape=[jax.ShapeDtypeStruct((t, D_MODEL), F32),
                   jax.ShapeDtypeStruct((t * ROW_SLABS, LANES), F32),
                   jax.ShapeDtypeStruct((t // tm, 2 * TOP_K, tm), I32),
                   jax.ShapeDtypeStruct((t, LANES), F32),
                   jax.ShapeDtypeStruct((SUBLANES, LANES), I32)],
        scratch_shapes=[pltpu.VMEM((SUBLANES, LANES), F32)],
        compiler_params=_params(("arbitrary",)),
        name="merge",
    )(x2d, ml, yf, gf, w_fo, w_o, g_ffn.reshape(1, D_MODEL), w_r, b_r)


def _row_slab(ref, row, base=0):
    return ref.at[pl.ds(pl.multiple_of(base + row * ROW_SLABS, ROW_SLABS), ROW_SLABS), :]


def _scatter_body(pad_start_ref, pad_ref, nu_ref, dest_ref, h2_ref, xs_hbm, zbuf, sem, zsem, *, n_tiles):
    tb = dest_ref.shape[-1] // TOP_K
    tile_rows = EXP_TILE * ROW_SLABS

    @pl.when(pl.program_id(0) == 0)
    def _():
        zbuf[...] = jnp.zeros_like(zbuf)

        def pad_pieces():
            for e in range(N_EXPERTS):
                pad = pad_ref[e]
                for bit in reversed(range(EXP_TILE.bit_length() - 1)):
                    size = 1 << bit
                    before = (pad >> (bit + 1)) << (bit + 1)
                    first = pl.multiple_of((pad_start_ref[e] + before) * ROW_SLABS, ROW_SLABS)
                    cp = pltpu.make_async_copy(zbuf.at[pl.ds(0, size * ROW_SLABS), :],
                                               xs_hbm.at[pl.ds(first, size * ROW_SLABS), :], zsem)
                    yield (pad & size) != 0, cp

        def tail_copy(j):
            return pltpu.make_async_copy(
                zbuf, xs_hbm.at[pl.ds(pl.multiple_of(j * tile_rows, tile_rows), tile_rows), :], zsem)

        for cond, cp in pad_pieces():
            pl.when(cond)(cp.start)
        lax.fori_loop(nu_ref[0], n_tiles, lambda j, c: (tail_copy(j).start(), c)[1], 0)
        for cond, cp in pad_pieces():
            pl.when(cond)(cp.wait)
        lax.fori_loop(nu_ref[0], n_tiles, lambda j, c: (tail_copy(j).wait(), c)[1], 0)

    def issue(j, carry):
        for kk in range(TOP_K):
            d = dest_ref[0, 0, kk * tb + j]
            pltpu.make_async_copy(_row_slab(h2_ref, j), _row_slab(xs_hbm, d), sem).start(
                priority=kk % 2)
        return carry

    lax.fori_loop(0, tb, issue, 0, unroll=ISSUE_UNROLL)
    n = tb * TOP_K * ROW_SLABS
    pltpu.make_async_copy(xs_hbm.at[pl.ds(0, n), :], xs_hbm.at[pl.ds(0, n), :], sem).wait()


def _scatter(dest3, pad_start, pad, n_used, h2s, n_tiles):
    nb, _, n = dest3.shape
    tb = n // TOP_K
    grid_spec = pltpu.PrefetchScalarGridSpec(
        num_scalar_prefetch=3,
        grid=(nb,),
        in_specs=[pl.BlockSpec((1, 1, n), lambda i, *_: (i, 0, 0), memory_space=pltpu.SMEM),
                  pl.BlockSpec((tb * ROW_SLABS, LANES), lambda i, *_: (i, 0))],
        out_specs=pl.BlockSpec(memory_space=pl.ANY),
        scratch_shapes=[pltpu.VMEM((EXP_TILE * ROW_SLABS, LANES), F32), pltpu.SemaphoreType.DMA(()),
                        pltpu.SemaphoreType.DMA(())],
    )
    return pl.pallas_call(
        functools.partial(_scatter_body, n_tiles=n_tiles),
        grid_spec=grid_spec,
        out_shape=jax.ShapeDtypeStruct((n_tiles * EXP_TILE * ROW_SLABS, LANES), F32),
        compiler_params=_params(("arbitrary",)),
        name="scatter",
    )(pad_start, pad, n_used, dest3, h2s)


def _expert_body(te_ref, first_ref, slot_ref, next_ref, nu_ref, xs_ref, wgu_hbm, bgu_ref, wd_hbm, bd_ref,
                 ys_ref, wgu_f, wd_f, wgu_bf, wd_bf, sems):
    i = pl.program_id(0)
    tm = xs_ref.shape[0] // ROW_SLABS

    def weight_copies(e, s):
        return (pltpu.make_async_copy(wgu_hbm.at[e], wgu_f.at[s], sems.at[0, s]),
                pltpu.make_async_copy(wd_hbm.at[e], wd_f.at[s], sems.at[1, s]))

    @pl.when(i == 0)
    def _():
        for cp in weight_copies(te_ref[0], 0):
            cp.start()

    @pl.when(first_ref[i] == 1)
    def _():
        s = slot_ref[i]
        for cp in weight_copies(te_ref[i], s):
            cp.wait()

        @pl.when(next_ref[i] >= 0)
        def _():
            for cp in weight_copies(next_ref[i], 1 - s):
                cp.start()

        wgu_bf[...] = wgu_f[s].astype(BF16)
        wd_bf[...] = wd_f[s].astype(BF16)

    @pl.when(i < nu_ref[0])
    def _():
        x = jnp.concatenate([xs_ref[pl.ds(c, tm, stride=ROW_SLABS), :] for c in range(ROW_SLABS)],
                            axis=1).astype(BF16)
        gu = jnp.dot(x, wgu_bf[...], preferred_element_type=F32) + bgu_ref[...]
        gate = jnp.minimum(gu[:, :D_FF], SWIGLU_LIMIT)
        up = jnp.clip(gu[:, D_FF:], -SWIGLU_LIMIT, SWIGLU_LIMIT)
        glu = gate * jax.nn.sigmoid(SWIGLU_ALPHA * gate)
        y = jnp.dot(((up + 1.0) * glu).astype(BF16), wd_bf[...],
                    preferred_element_type=F32) + bd_ref[...]
        for c in range(ROW_SLABS):
            ys_ref[pl.ds(c, tm, stride=ROW_SLABS), :] = y[:, c * LANES:(c + 1) * LANES]


def _experts(tile_expert, n_used, tile_end, xs, w_gate_up, b_gate_up, w_down, b_down, n_tiles):
    tm = EXP_TILE
    tile = jnp.arange(n_tiles, dtype=I32)
    used = tile < n_used[0]
    prev = jnp.concatenate([jnp.full((1,), -1, I32), tile_expert[:-1]])
    first = jnp.logical_and(used, tile_expert != prev).astype(I32)
    slot = ((jnp.cumsum(first) - 1) % 2).astype(I32)
    next_tile = tile_end[tile_expert].astype(I32)
    nxt = jnp.where(next_tile < n_used[0], tile_expert[jnp.minimum(next_tile, n_tiles - 1)], -1).astype(I32)

    rows = lambda i, te, fi, sl, nx, nu: (jnp.minimum(i, nu[0] - 1), 0)
    bsel = lambda i, te, fi, sl, nx, nu: (te[i], 0, 0)
    grid_spec = pltpu.PrefetchScalarGridSpec(
        num_scalar_prefetch=5,
        grid=(n_tiles,),
        in_specs=[pl.BlockSpec((tm * ROW_SLABS, LANES), rows),
                  pl.BlockSpec(memory_space=pl.ANY),
                  pl.BlockSpec((None, 1, 2 * D_FF), bsel),
                  pl.BlockSpec(memory_space=pl.ANY),
                  pl.BlockSpec((None, 1, D_MODEL), bsel)],
        out_specs=pl.BlockSpec((tm * ROW_SLABS, LANES), rows),
        scratch_shapes=[pltpu.VMEM((2, D_MODEL, 2 * D_FF), F32), pltpu.VMEM((2, D_FF, D_MODEL), F32),
                        pltpu.VMEM((D_MODEL, 2 * D_FF), BF16), pltpu.VMEM((D_FF, D_MODEL), BF16),
                        pltpu.SemaphoreType.DMA((2, 2))],
    )
    return pl.pallas_call(
        _expert_body,
        grid_spec=grid_spec,
        out_shape=jax.ShapeDtypeStruct(xs.shape, F32),
        input_output_aliases={5: 0},
        compiler_params=_params(("arbitrary",)),
        name="experts",
    )(tile_expert, first, slot, nxt, n_used, xs, w_gate_up, b_gate_up.reshape(N_EXPERTS, 1, 2 * D_FF),
      w_down, b_down.reshape(N_EXPERTS, 1, D_MODEL))


def _combine_body(dcur_ref, dnext_ref, x2_ref, mw_ref, g_ref, ys_hbm, o_ref, buf, sem):
    i = pl.program_id(0)
    tb = x2_ref.shape[0]
    slot_rows = TOP_K * tb * ROW_SLABS
    cur = i % 2

    def issue(d_ref, slot):
        base = pl.multiple_of(slot * slot_rows, slot_rows)

        def body(j, carry):
            for kk in range(TOP_K):
                d = d_ref[0, 0, kk * tb + j]
                pltpu.make_async_copy(_row_slab(ys_hbm, d), _row_slab(buf, kk * tb + j, base),
                                      sem.at[slot]).start(priority=kk % 2)
            return carry

        lax.fori_loop(0, tb, body, 0, unroll=ISSUE_UNROLL)

    @pl.when(i == 0)
    def _():
        issue(dcur_ref, 0)

    @pl.when(i + 1 < pl.num_programs(0))
    def _():
        issue(dnext_ref, 1 - cur)

    base = pl.multiple_of(cur * slot_rows, slot_rows)
    pltpu.make_async_copy(ys_hbm.at[pl.ds(0, slot_rows), :], buf.at[pl.ds(base, slot_rows), :],
                          sem.at[cur]).wait()

    mw = mw_ref[...]
    wk = [jnp.broadcast_to(mw[:, kk:kk + 1], (tb, LANES)) for kk in range(TOP_K)]
    cols = []
    for c in range(ROW_SLABS):
        acc = x2_ref[:, c * LANES:(c + 1) * LANES]
        for kk in range(TOP_K):
            rows = buf[pl.ds(base + kk * tb * ROW_SLABS + c, tb, stride=ROW_SLABS), :]
            acc = acc + wk[kk] * rows
        cols.append(acc)
    o_ref[...] = _rms(jnp.concatenate(cols, axis=1), g_ref[...])


def _combine(dest3, x2, mw, g_final, ys):
    t = x2.shape[0]
    nb, _, n = dest3.shape
    tb = n // TOP_K
    row = lambda i: (i, 0)
    dspec = lambda imap: pl.BlockSpec((1, 1, n), imap, memory_space=pltpu.SMEM)
    return pl.pallas_call(
        _combine_body,
        grid=(nb,),
        in_specs=[dspec(lambda i: (i, 0, 0)), dspec(lambda i: (jnp.minimum(i + 1, nb - 1), 0, 0)),
                  pl.BlockSpec((tb, D_MODEL), row), pl.BlockSpec((tb, LANES), row),
                  _const_spec((1, D_MODEL)), pl.BlockSpec(memory_space=pl.ANY)],
        out_specs=pl.BlockSpec((tb, D_MODEL), row),
        out_shape=jax.ShapeDtypeStruct((t, D_MODEL), F32),
        scratch_shapes=[pltpu.VMEM((2 * TOP_K * tb * ROW_SLABS, LANES), F32),
                        pltpu.SemaphoreType.DMA((2,))],
        compiler_params=_params(("arbitrary",)),
        name="combine",
    )(dest3, dest3, x2, mw, g_final.reshape(1, D_MODEL), ys)


def _block_diag_gates(w_rg, w_ig):
    per = GATE_CHUNK // RNN_BLOCK_DIM
    nchunk = w_rg.shape[0] // per
    eye = jnp.eye(per, dtype=w_rg.dtype)

    def bd(w):
        w = w.reshape(nchunk, per, RNN_BLOCK_DIM, RNN_BLOCK_DIM)
        return jnp.einsum('cpij,pq->cpiqj', w, eye).reshape(nchunk, GATE_CHUNK, GATE_CHUNK)

    return jnp.concatenate([bd(w_rg), bd(w_ig)], axis=-1).astype(BF16)


def kernel(x, g_mix, w_in, conv_w, conv_b, w_rg, b_rg, w_ig, b_ig, lru_lambda, b_f, w_lru_out,
           w_fox_out, w_o, g_ffn, w_router, b_router, w_gate_up, b_gate_up, w_down, b_down, g_final):
    bsz, s, d = x.shape
    assert d == D_MODEL
    t = bsz * s
    x2d = x.reshape(t, d)

    o_f = 5 * D_MODEL
    o_g = o_f + N_HEADS
    w_wide = jnp.concatenate([w_in[:, :o_f], w_in[:, o_g:]], axis=1).astype(BF16)
    w_f = jnp.pad(w_in[:, o_f:o_g], ((0, 0), (0, LANES - N_HEADS))).astype(BF16)
    b_f_pad = jnp.pad(b_f, (0, LANES - N_HEADS)).reshape(1, LANES)
    w_r = jnp.pad(w_router, ((0, 0), (0, LANES - N_EXPERTS)))
    w_r_hi = w_r.astype(BF16)
    w_r = jnp.stack([w_r_hi, (w_r - w_r_hi.astype(F32)).astype(BF16)])
    b_r = jnp.pad(b_router, (0, LANES - N_EXPERTS)).reshape(1, LANES)

    lx, lg, q, k, v, gl, gf, cum = _inproj(x2d, g_mix, w_wide, w_f, b_f_pad, s)

    ml = _lru(lx, lg, gl, conv_w, conv_b, _block_diag_gates(w_rg, w_ig), b_rg, b_ig, lru_lambda,
              w_lru_out.astype(BF16), bsz, s)

    ct = cum[:, :N_HEADS].reshape(bsz, s, N_HEADS // HEADS_PER_STEP, HEADS_PER_STEP).transpose(0, 2, 3, 1)
    yf = _attention(q, k, v, ct, bsz, s)

    x2, h2s, route, mw, cnt = _merge(x2d, ml, yf, gf, w_fox_out.astype(BF16), w_o.astype(BF16),
                                     g_ffn, w_r, b_r)

    assert TOK_TILE == ROW_TILE
    counts = cnt[0, :N_EXPERTS]
    tiles_e = (counts + EXP_TILE - 1) // EXP_TILE
    tile_end = jnp.cumsum(tiles_e)
    row_off = (tile_end - tiles_e) * EXP_TILE
    idx = route[:, :TOP_K, :]
    dest = route[:, TOP_K:, :]
    for e in range(N_EXPERTS):
        dest = dest + jnp.where(idx == e, row_off[e], 0)
    dest3 = dest.astype(I32).reshape(route.shape[0], 1, TOP_K * route.shape[2])
    n_tiles = (t * TOP_K) // EXP_TILE + N_EXPERTS
    n_used = tile_end[-1:].astype(I32)
    tile_ids = jnp.minimum(jnp.arange(n_tiles, dtype=I32), n_used[0] - 1)
    tile_expert = jnp.sum(tile_ids[:, None] >= tile_end[None, :], axis=1).astype(I32)

    xs = _scatter(dest3, (row_off + counts).astype(I32), (tiles_e * EXP_TILE - counts).astype(I32),
                  n_used, h2s, n_tiles)
    ys = _experts(tile_expert, n_used, tile_end, xs, w_gate_up, b_gate_up, w_down, b_down, n_tiles)
    out = _combine(dest3, x2, mw, g_final, ys)
    return out.reshape(bsz, s, d)
```

```python
import functools
import math

import jax
import jax.numpy as jnp
from jax import lax
from jax.experimental import pallas as pl
from jax.experimental.pallas import tpu as pltpu

F32 = jnp.float32
BF16 = jnp.bfloat16
I32 = jnp.int32

D_MODEL = 1024
RNN_BLOCK_DIM = 64
LRU_C = 8.0
HEAD_DIM = 64
N_HEADS = 16
N_EXPERTS = 32
TOP_K = 4
D_FF = 1024
SWIGLU_LIMIT = 7.0
SWIGLU_ALPHA = 1.702
RMS_EPS = 1e-6
LOG2E = 1.4426950408889634
Q_SCALE = LOG2E / math.sqrt(HEAD_DIM)
Q_GROUP = 2

LANES = 128
SUBLANES = 8
ROW_SLABS = D_MODEL // LANES
VMEM_LIMIT_BYTES = 56 * 1024 * 1024

PROJ_TILE = 256
TOK_TILE = 512
LRU_TILE = 512
ATT_TILE = 512
HEADS_PER_STEP = 4
EXP_TILE = 256
ROW_TILE = 512
ISSUE_UNROLL = 4
CONV_WIDTH = 4
GATE_CHUNK = 256


def _params(semantics):
    return pltpu.CompilerParams(dimension_semantics=semantics, vmem_limit_bytes=VMEM_LIMIT_BYTES)


def _rms(x, g):
    return x * lax.rsqrt(jnp.mean(x * x, axis=-1, keepdims=True) + RMS_EPS) * g


def _const_spec(shape):
    nd = len(shape)
    return pl.BlockSpec(shape, lambda *_: (0,) * nd)


def _inproj_body(x_ref, g_ref, w_ref, wf_ref, bf_ref, *rest, steps_per_seq):
    *wide, o_cum, carry = rest
    tm = x_ref.shape[0]

    @pl.when(pl.program_id(0) % steps_per_seq == 0)
    def _():
        carry[...] = jnp.zeros_like(carry)

    h = _rms(x_ref[...], g_ref[...]).astype(BF16)
    for j, o in enumerate(wide):
        acc = jnp.dot(h, w_ref[:, j * D_MODEL:(j + 1) * D_MODEL], preferred_element_type=F32)
        if j == Q_GROUP:
            acc = acc * Q_SCALE
        o[...] = acc.astype(o.dtype)

    z = jnp.dot(h, wf_ref[...], preferred_element_type=F32) + bf_ref[...]
    logf = jnp.minimum(z, 0.0) - jnp.log1p(jnp.exp(-jnp.abs(z)))
    r = lax.broadcasted_iota(I32, (tm, tm), 0)
    c = lax.broadcasted_iota(I32, (tm, tm), 1)
    tri = (r >= c).astype(BF16)
    cum = carry[0:1, :]
    for term in _split3(logf):
        cum = cum + jnp.dot(tri, term.astype(BF16), preferred_element_type=F32)
    o_cum[...] = cum
    carry[...] = jnp.broadcast_to(cum[tm - 1:tm, :], carry.shape)


def _inproj(x2d, g_mix, w_wide, w_f, b_f_pad, s):
    t = x2d.shape[0]
    n_wide = w_wide.shape[1] // D_MODEL
    tm = min(PROJ_TILE, s)
    row = lambda i: (i, 0)
    return pl.pallas_call(
        functools.partial(_inproj_body, steps_per_seq=s // tm),
        grid=(t // tm,),
        in_specs=[pl.BlockSpec((tm, D_MODEL), row), _const_spec((1, D_MODEL)),
                  _const_spec(w_wide.shape), _const_spec(w_f.shape), _const_spec((1, LANES))],
        out_specs=[pl.BlockSpec((tm, D_MODEL), row)] * n_wide + [pl.BlockSpec((tm, LANES), row)],
        out_shape=[jax.ShapeDtypeStruct((t, D_MODEL), BF16)] * n_wide
                  + [jax.ShapeDtypeStruct((t, LANES), F32)],
        scratch_shapes=[pltpu.VMEM((SUBLANES, LANES), F32)],
        compiler_params=_params(("arbitrary",)),
        name="inproj",
    )(x2d, g_mix.reshape(1, D_MODEL), w_wide, w_f, b_f_pad)


def _lru_body(lx_ref, lg_ref, gl_ref, cw_ref, cb_ref, wbd_ref, brg_ref, big_ref, lam_ref, wo_ref,
              o_ref, xbuf, hcar, a_s, b_s, h_s):
    ts = lx_ref.shape[0]

    @pl.when(pl.program_id(1) == 0)
    def _():
        xbuf[0:SUBLANES, :] = jnp.zeros((SUBLANES, D_MODEL), F32)
        hcar[...] = jnp.zeros_like(hcar)

    x = lx_ref[...].astype(F32)
    xbuf[SUBLANES:SUBLANES + ts, :] = x
    xc = cb_ref[...] + cw_ref[CONV_WIDTH - 1:CONV_WIDTH, :] * x
    for k in range(CONV_WIDTH - 1):
        off = SUBLANES - (CONV_WIDTH - 1) + k
        xc = xc + cw_ref[k:k + 1, :] * xbuf[off:off + ts, :]
    xbuf[0:SUBLANES, :] = xbuf[ts:ts + SUBLANES, :]

    xcb = xc.astype(BF16)
    for c in range(D_MODEL // GATE_CHUNK):
        sl = slice(c * GATE_CHUNK, (c + 1) * GATE_CHUNK)
        g = jnp.dot(xcb[:, sl], wbd_ref[c], preferred_element_type=F32)
        rt = jax.nn.sigmoid(g[:, :GATE_CHUNK] + brg_ref[:, sl])
        it = jax.nn.sigmoid(g[:, GATE_CHUNK:] + big_ref[:, sl])
        nl = -lam_ref[:, sl]
        softplus = jnp.maximum(nl, 0.0) + jnp.log1p(jnp.exp(-jnp.abs(nl)))
        a = jnp.exp(-LRU_C * rt * softplus)
        a_s[:, sl] = a
        b_s[:, sl] = jnp.sqrt(1.0 - a * a) * (it * xc[:, sl])

    row = lax.broadcasted_iota(I32, (SUBLANES, D_MODEL), 0)

    def group(gidx, h0):
        off = pl.multiple_of(gidx * SUBLANES, SUBLANES)
        av = a_s[pl.ds(off, SUBLANES), :]
        bv = b_s[pl.ds(off, SUBLANES), :]
        for d in (1, 2, 4):
            keep = row >= d
            a_sh = jnp.where(keep, pltpu.roll(av, d, 0), 1.0)
            b_sh = jnp.where(keep, pltpu.roll(bv, d, 0), 0.0)
            bv = av * b_sh + bv
            av = av * a_sh
        hv = av * h0 + bv
        h_s[pl.ds(off, SUBLANES), :] = hv
        return jnp.broadcast_to(hv[SUBLANES - 1:SUBLANES, :], (SUBLANES, D_MODEL))

    hcar[...] = lax.fori_loop(0, ts // SUBLANES, group, hcar[...])

    y = (h_s[...] * jax.nn.gelu(lg_ref[...].astype(F32), approximate=True)).astype(BF16)
    proj = jnp.dot(y, wo_ref[...], preferred_element_type=F32)
    o_ref[...] = (jax.nn.sigmoid(gl_ref[...].astype(F32)) * proj).astype(o_ref.dtype)


def _lru(lx, lg, gl, conv_w, conv_b, wbd, b_rg, b_ig, lam, w_out, bsz, s):
    ts = min(LRU_TILE, s)
    ns = s // ts
    tile = pl.BlockSpec((ts, D_MODEL), lambda b, i: (b * ns + i, 0))
    vec = lambda a: a.reshape(1, D_MODEL)
    return pl.pallas_call(
        _lru_body,
        grid=(bsz, ns),
        in_specs=[tile, tile, tile, _const_spec((CONV_WIDTH, D_MODEL)), _const_spec((1, D_MODEL)),
                  _const_spec(wbd.shape), _const_spec((1, D_MODEL)), _const_spec((1, D_MODEL)),
                  _const_spec((1, D_MODEL)), _const_spec(w_out.shape)],
        out_specs=tile,
        out_shape=jax.ShapeDtypeStruct((bsz * s, D_MODEL), BF16),
        scratch_shapes=[pltpu.VMEM((ts + SUBLANES, D_MODEL), F32), pltpu.VMEM((SUBLANES, D_MODEL), F32),
                        pltpu.VMEM((ts, D_MODEL), F32), pltpu.VMEM((ts, D_MODEL), F32),
                        pltpu.VMEM((ts, D_MODEL), F32)],
        compiler_params=_params(("arbitrary", "arbitrary")),
        name="lru",
    )(lx, lg, gl, conv_w, vec(conv_b), wbd, vec(b_rg), vec(b_ig), vec(lam), w_out)


def _split3(c):
    hi = c.astype(BF16).astype(F32)
    rest = c - hi
    mid = rest.astype(BF16).astype(F32)
    return hi, mid, rest - mid


def _bias_columns(row_ref, h, start, n, query_side):
    hi, mid, lo = _split3(row_ref[h:h + 1, pl.ds(start, n)] * LOG2E)
    one = jnp.ones_like(hi)
    terms = (hi, mid, lo, one, one, one) if query_side else (one, one, one, -hi, -mid, -lo)
    sub = lax.broadcasted_iota(I32, (SUBLANES, n), 0)
    rows = jnp.zeros((SUBLANES, n), F32)
    for r, term in enumerate(terms):
        rows = jnp.where(sub == r, term, rows)
    above = jnp.zeros((HEAD_DIM, LANES), F32)
    below = jnp.zeros((LANES - HEAD_DIM - SUBLANES, LANES), F32)
    blocks = [jnp.concatenate([above, rows[:, j * LANES:(j + 1) * LANES], below], axis=0).T
              for j in range(n // LANES)]
    return jnp.concatenate(blocks, axis=0)


def _attn_body(q_ref, k_ref, v_ref, cq_ref, ck_ref, o_ref, ka_s, va_s, qa_s, m_s, acc_s):
    tq = q_ref.shape[0]
    tk = tq
    s_len = k_ref.shape[0]
    qi = pl.program_id(2)
    lane = lax.broadcasted_iota(I32, (1, LANES), 1)
    is_head = lane < HEAD_DIM
    ones_col = jnp.where(lane == HEAD_DIM, 1.0, 0.0)

    def head_lanes(x, h):
        g = x[:, (h // 2) * LANES:(h // 2 + 1) * LANES]
        return g if h % 2 == 0 else pltpu.roll(g, HEAD_DIM, 1)

    @pl.when(qi == 0)
    def _():
        def chunk(ci, carry):
            off = pl.multiple_of(ci * tk, tk)
            k2 = k_ref[pl.ds(off, tk), :].astype(F32)
            v2 = v_ref[pl.ds(off, tk), :].astype(F32)
            for h in range(HEADS_PER_STEP):
                ext = _bias_columns(ck_ref, h, off, tk, query_side=False)
                ka_s[h, pl.ds(off, tk), :] = jnp.where(is_head, head_lanes(k2, h), ext).astype(BF16)
                va_s[h, pl.ds(off, tk), :] = jnp.where(is_head, head_lanes(v2, h), ones_col).astype(BF16)
            return carry

        lax.fori_loop(0, s_len // tk, chunk, 0)

    q2 = q_ref[...].astype(F32)
    for h in range(HEADS_PER_STEP):
        ext = _bias_columns(cq_ref, h, 0, tq, query_side=True)
        qa_s[h] = jnp.where(is_head, head_lanes(q2, h), ext).astype(BF16)
    m_s[...] = jnp.full_like(m_s, -jnp.inf)
    acc_s[...] = jnp.zeros_like(acc_s)

    def step(first_key, nk, diag_shift):
        off = pl.multiple_of(first_key, tk)
        for h in range(HEADS_PER_STEP):
            sc = lax.dot_general(qa_s[h], ka_s[h, pl.ds(off, nk), :], (((1,), (1,)), ((), ())),
                                 preferred_element_type=F32)
            if diag_shift is not None:
                r = lax.broadcasted_iota(I32, (tq, nk), 0)
                c = lax.broadcasted_iota(I32, (tq, nk), 1)
                sc = jnp.where(c <= r + diag_shift, sc, -jnp.inf)
            m_prev = m_s[h]
            m_new = jnp.maximum(m_prev, jnp.max(sc, axis=1, keepdims=True))
            p = jnp.concatenate([jnp.exp2(sc[:, j * LANES:(j + 1) * LANES] - m_new)
                                 for j in range(nk // LANES)], axis=1)
            m_s[h] = m_new
            acc_s[h] = jnp.exp2(m_prev - m_new) * acc_s[h] + jnp.dot(
                p.astype(BF16), va_s[h, pl.ds(off, nk), :], preferred_element_type=F32)

    def wide_step(j, carry):
        step(j * (2 * tk), 2 * tk, None)
        return carry

    lax.fori_loop(0, qi // 2, wide_step, 0)

    @pl.when(qi % 2 == 1)
    def _():
        step((qi - 1) * tk, 2 * tk, tk)

    @pl.when(qi % 2 == 0)
    def _():
        step(qi * tk, tk, 0)

    groups = []
    for g in range(HEADS_PER_STEP // 2):
        o0 = acc_s[2 * g] / acc_s[2 * g][:, HEAD_DIM:HEAD_DIM + 1]
        o1 = acc_s[2 * g + 1] / acc_s[2 * g + 1][:, HEAD_DIM:HEAD_DIM + 1]
        groups.append(jnp.where(is_head, o0, pltpu.roll(o1, HEAD_DIM, 1)))
    o_ref[...] = jnp.concatenate(groups, axis=1).astype(o_ref.dtype)


def _attention(q, k, v, ct, bsz, s):
    tq = min(ATT_TILE, s)
    nq = s // tq
    hps = HEADS_PER_STEP
    width = hps * HEAD_DIM
    qspec = pl.BlockSpec((tq, width), lambda b, p, i: (b * nq + i, p))
    kvspec = pl.BlockSpec((s, width), lambda b, p, i: (b, p))
    return pl.pallas_call(
        _attn_body,
        grid=(bsz, N_HEADS // hps, nq),
        in_specs=[qspec, kvspec, kvspec,
                  pl.BlockSpec((None, None, hps, tq), lambda b, p, i: (b, p, 0, i)),
                  pl.BlockSpec((None, None, hps, s), lambda b, p, i: (b, p, 0, 0))],
        out_specs=qspec,
        out_shape=jax.ShapeDtypeStruct((bsz * s, D_MODEL), BF16),
        scratch_shapes=[pltpu.VMEM((hps, s, LANES), BF16), pltpu.VMEM((hps, s, LANES), BF16),
                        pltpu.VMEM((hps, tq, LANES), BF16), pltpu.VMEM((hps, tq, LANES), F32),
                        pltpu.VMEM((hps, tq, LANES), F32)],
        compiler_params=_params(("arbitrary", "arbitrary", "arbitrary")),
        name="attn",
    )(q, k, v, ct, ct)


def _merge_body(x_ref, ml_ref, yf_ref, gf_ref, wfo_ref, wo_ref, g_ref, wr_ref, br_ref,
                x2_ref, h2_ref, mi_ref, mw_ref, cnt_ref, running):
    tm = x_ref.shape[0]

    @pl.when(pl.program_id(0) == 0)
    def _():
        running[...] = jnp.zeros_like(running)

    fox = jnp.dot(yf_ref[...], wfo_ref[...], preferred_element_type=F32)
    merged = ml_ref[...].astype(F32) + jax.nn.sigmoid(gf_ref[...].astype(F32)) * fox
    x2 = x_ref[...] + jnp.dot(merged.astype(BF16), wo_ref[...], preferred_element_type=F32)
    x2_ref[...] = x2
    h2 = _rms(x2, g_ref[...])
    for c in range(ROW_SLABS):
        h2_ref[pl.ds(c, tm, stride=ROW_SLABS), :] = h2[:, c * LANES:(c + 1) * LANES]

    h_hi = h2.astype(BF16)
    h_lo = (h2 - h_hi.astype(F32)).astype(BF16)
    logits = (jnp.dot(h_hi, wr_ref[0], preferred_element_type=F32)
              + jnp.dot(h_lo, wr_ref[0], preferred_element_type=F32)
              + jnp.dot(h_hi, wr_ref[1], preferred_element_type=F32)) + br_ref[...]
    lane = lax.broadcasted_iota(I32, (tm, LANES), 1)
    lane_f = lane.astype(F32)
    logits = jnp.where(lane < N_EXPERTS, logits, -jnp.inf)
    vals, idxs = [], []
    for _ in range(TOP_K):
        mx = jnp.max(logits, axis=1, keepdims=True)
        ix = jnp.min(jnp.where(logits == mx, lane_f, float(LANES)), axis=1, keepdims=True).astype(I32)
        vals.append(mx)
        idxs.append(ix)
        logits = jnp.where(lane == ix, -jnp.inf, logits)
    exps = [jnp.exp(vk - vals[0]) for vk in vals]
    denom = exps[0] + exps[1] + exps[2] + exps[3]

    onehot = jnp.zeros((tm, LANES), F32)
    for ix in idxs:
        onehot = onehot + (lane == ix).astype(F32)
    r = lax.broadcasted_iota(I32, (tm, tm), 0)
    c = lax.broadcasted_iota(I32, (tm, tm), 1)
    before = jnp.dot((c < r).astype(BF16), onehot.astype(BF16), preferred_element_type=F32)
    pos = before + running[0:1, :]
    mi = jnp.zeros((tm, LANES), F32)
    mw = jnp.zeros((tm, LANES), F32)
    for kk in range(TOP_K):
        rank = jnp.sum(jnp.where(lane == idxs[kk], pos, 0.0), axis=1, keepdims=True)
        mi = jnp.where(lane == kk, idxs[kk].astype(F32), mi)
        mi = jnp.where(lane == TOP_K + kk, rank, mi)
        mw = jnp.where(lane == kk, exps[kk] / denom, mw)
    mi_ref[...] = mi.T[0:2 * TOP_K, :].astype(I32)
    mw_ref[...] = mw
    total = running[0:1, :] + jnp.sum(onehot, axis=0, keepdims=True)
    running[...] = jnp.broadcast_to(total, running.shape)
    cnt_ref[...] = jnp.broadcast_to(total, cnt_ref.shape).astype(I32)


def _merge(x2d, ml, yf, gf, w_fo, w_o, g_ffn, w_r, b_r):
    t = x2d.shape[0]
    tm = min(TOK_TILE, t)
    row = lambda i: (i, 0)
    tile = pl.BlockSpec((tm, D_MODEL), row)
    meta = pl.BlockSpec((tm, LANES), row)
    return pl.pallas_call(
        _merge_body,
        grid=(t // tm,),
        in_specs=[tile, tile, tile, tile, _const_spec(w_fo.shape), _const_spec(w_o.shape),
                  _const_spec((1, D_MODEL)), _const_spec(w_r.shape), _const_spec((1, LANES))],
        out_specs=[tile, pl.BlockSpec((tm * ROW_SLABS, LANES), row),
                   pl.BlockSpec((None, 2 * TOP_K, tm), lambda i: (i, 0, 0)), meta,
                   _const_spec((SUBLANES, LANES))],
        out_shape=[jax.ShapeDtypeStruct((t, D_MODEL), F32),
                   jax.ShapeDtypeStruct((t * ROW_SLABS, LANES), F32),
                   jax.ShapeDtypeStruct((t // tm, 2 * TOP_K, tm), I32),
                   jax.ShapeDtypeStruct((t, LANES), F32),
                   jax.ShapeDtypeStruct((SUBLANES, LANES), I32)],
        scratch_shapes=[pltpu.VMEM((SUBLANES, LANES), F32)],
        compiler_params=_params(("arbitrary",)),
        name="merge",
    )(x2d, ml, yf, gf, w_fo, w_o, g_ffn.reshape(1, D_MODEL), w_r, b_r)


def _row_slab(ref, row, base=0):
    return ref.at[pl.ds(pl.multiple_of(base + row * ROW_SLABS, ROW_SLABS), ROW_SLABS), :]


def _scatter_body(pad_start_ref, pad_ref, nu_ref, dest_ref, h2_ref, xs_hbm, zbuf, sem, zsem, *, n_tiles):
    tb = dest_ref.shape[-1] // TOP_K
    tile_rows = EXP_TILE * ROW_SLABS

    @pl.when(pl.program_id(0) == 0)
    def _():
        zbuf[...] = jnp.zeros_like(zbuf)

        def pad_pieces():
            for e in range(N_EXPERTS):
                pad = pad_ref[e]
                for bit in reversed(range(EXP_TILE.bit_length() - 1)):
                    size = 1 << bit
                    before = (pad >> (bit + 1)) << (bit + 1)
                    first = pl.multiple_of((pad_start_ref[e] + before) * ROW_SLABS, ROW_SLABS)
                    cp = pltpu.make_async_copy(zbuf.at[pl.ds(0, size * ROW_SLABS), :],
                                               xs_hbm.at[pl.ds(first, size * ROW_SLABS), :], zsem)
                    yield (pad & size) != 0, cp

        def tail_copy(j):
            return pltpu.make_async_copy(
                zbuf, xs_hbm.at[pl.ds(pl.multiple_of(j * tile_rows, tile_rows), tile_rows), :], zsem)

        for cond, cp in pad_pieces():
            pl.when(cond)(cp.start)
        lax.fori_loop(nu_ref[0], n_tiles, lambda j, c: (tail_copy(j).start(), c)[1], 0)
        for cond, cp in pad_pieces():
            pl.when(cond)(cp.wait)
        lax.fori_loop(nu_ref[0], n_tiles, lambda j, c: (tail_copy(j).wait(), c)[1], 0)

    def issue(j, carry):
        for kk in range(TOP_K):
            d = dest_ref[0, 0, kk * tb + j]
            pltpu.make_async_copy(_row_slab(h2_ref, j), _row_slab(xs_hbm, d), sem).start(
                priority=kk % 2)
        return carry

    lax.fori_loop(0, tb, issue, 0, unroll=ISSUE_UNROLL)
    n = tb * TOP_K * ROW_SLABS
    pltpu.make_async_copy(xs_hbm.at[pl.ds(0, n), :], xs_hbm.at[pl.ds(0, n), :], sem).wait()


def _scatter(dest3, pad_start, pad, n_used, h2s, n_tiles):
    nb, _, n = dest3.shape
    tb = n // TOP_K
    grid_spec = pltpu.PrefetchScalarGridSpec(
        num_scalar_prefetch=3,
        grid=(nb,),
        in_specs=[pl.BlockSpec((1, 1, n), lambda i, *_: (i, 0, 0), memory_space=pltpu.SMEM),
                  pl.BlockSpec((tb * ROW_SLABS, LANES), lambda i, *_: (i, 0))],
        out_specs=pl.BlockSpec(memory_space=pl.ANY),
        scratch_shapes=[pltpu.VMEM((EXP_TILE * ROW_SLABS, LANES), F32), pltpu.SemaphoreType.DMA(()),
                        pltpu.SemaphoreType.DMA(())],
    )
    return pl.pallas_call(
        functools.partial(_scatter_body, n_tiles=n_tiles),
        grid_spec=grid_spec,
        out_shape=jax.ShapeDtypeStruct((n_tiles * EXP_TILE * ROW_SLABS, LANES), F32),
        compiler_params=_params(("arbitrary",)),
        name="scatter",
    )(pad_start, pad, n_used, dest3, h2s)


def _expert_body(te_ref, first_ref, slot_ref, next_ref, nu_ref, xs_ref, wgu_hbm, bgu_ref, wd_hbm, bd_ref,
                 ys_ref, wgu_f, wd_f, wgu_bf, wd_bf, sems):
    i = pl.program_id(0)
    tm = xs_ref.shape[0] // ROW_SLABS

    def weight_copies(e, s):
        return (pltpu.make_async_copy(wgu_hbm.at[e], wgu_f.at[s], sems.at[0, s]),
                pltpu.make_async_copy(wd_hbm.at[e], wd_f.at[s], sems.at[1, s]))

    @pl.when(i == 0)
    def _():
        for cp in weight_copies(te_ref[0], 0):
            cp.start()

    @pl.when(first_ref[i] == 1)
    def _():
        s = slot_ref[i]
        for cp in weight_copies(te_ref[i], s):
            cp.wait()

        @pl.when(next_ref[i] >= 0)
        def _():
            for cp in weight_copies(next_ref[i], 1 - s):
                cp.start()

        wgu_bf[...] = wgu_f[s].astype(BF16)
        wd_bf[...] = wd_f[s].astype(BF16)

    @pl.when(i < nu_ref[0])
    def _():
        x = jnp.concatenate([xs_ref[pl.ds(c, tm, stride=ROW_SLABS), :] for c in range(ROW_SLABS)],
                            axis=1).astype(BF16)
        gu = jnp.dot(x, wgu_bf[...], preferred_element_type=F32) + bgu_ref[...]
        gate = jnp.minimum(gu[:, :D_FF], SWIGLU_LIMIT)
        up = jnp.clip(gu[:, D_FF:], -SWIGLU_LIMIT, SWIGLU_LIMIT)
        glu = gate * jax.nn.sigmoid(SWIGLU_ALPHA * gate)
        y = jnp.dot(((up + 1.0) * glu).astype(BF16), wd_bf[...],
                    preferred_element_type=F32) + bd_ref[...]
        for c in range(ROW_SLABS):
            ys_ref[pl.ds(c, tm, stride=ROW_SLABS), :] = y[:, c * LANES:(c + 1) * LANES]


def _experts(tile_expert, n_used, tile_end, xs, w_gate_up, b_gate_up, w_down, b_down, n_tiles):
    tm = EXP_TILE
    tile = jnp.arange(n_tiles, dtype=I32)
    used = tile < n_used[0]
    prev = jnp.concatenate([jnp.full((1,), -1, I32), tile_expert[:-1]])
    first = jnp.logical_and(used, tile_expert != prev).astype(I32)
    slot = ((jnp.cumsum(first) - 1) % 2).astype(I32)
    next_tile = tile_end[tile_expert].astype(I32)
    nxt = jnp.where(next_tile < n_used[0], tile_expert[jnp.minimum(next_tile, n_tiles - 1)], -1).astype(I32)

    rows = lambda i, te, fi, sl, nx, nu: (jnp.minimum(i, nu[0] - 1), 0)
    bsel = lambda i, te, fi, sl, nx, nu: (te[i], 0, 0)
    grid_spec = pltpu.PrefetchScalarGridSpec(
        num_scalar_prefetch=5,
        grid=(n_tiles,),
        in_specs=[pl.BlockSpec((tm * ROW_SLABS, LANES), rows),
                  pl.BlockSpec(memory_space=pl.ANY),
                  pl.BlockSpec((None, 1, 2 * D_FF), bsel),
                  pl.BlockSpec(memory_space=pl.ANY),
                  pl.BlockSpec((None, 1, D_MODEL), bsel)],
        out_specs=pl.BlockSpec((tm * ROW_SLABS, LANES), rows),
        scratch_shapes=[pltpu.VMEM((2, D_MODEL, 2 * D_FF), F32), pltpu.VMEM((2, D_FF, D_MODEL), F32),
                        pltpu.VMEM((D_MODEL, 2 * D_FF), BF16), pltpu.VMEM((D_FF, D_MODEL), BF16),
                        pltpu.SemaphoreType.DMA((2, 2))],
    )
    return pl.pallas_call(
        _expert_body,
        grid_spec=grid_spec,
        out_shape=jax.ShapeDtypeStruct(xs.shape, F32),
        input_output_aliases={5: 0},
        compiler_params=_params(("arbitrary",)),
        name="experts",
    )(tile_expert, first, slot, nxt, n_used, xs, w_gate_up, b_gate_up.reshape(N_EXPERTS, 1, 2 * D_FF),
      w_down, b_down.reshape(N_EXPERTS, 1, D_MODEL))


def _combine_body(dcur_ref, dnext_ref, x2_ref, mw_ref, g_ref, ys_hbm, o_ref, buf, sem):
    i = pl.program_id(0)
    tb = x2_ref.shape[0]
    slot_rows = TOP_K * tb * ROW_SLABS
    cur = i % 2

    def issue(d_ref, slot):
        base = pl.multiple_of(slot * slot_rows, slot_rows)

        def body(j, carry):
            for kk in range(TOP_K):
                d = d_ref[0, 0, kk * tb + j]
                pltpu.make_async_copy(_row_slab(ys_hbm, d), _row_slab(buf, kk * tb + j, base),
                                      sem.at[slot]).start(priority=kk % 2)
            return carry

        lax.fori_loop(0, tb, body, 0, unroll=ISSUE_UNROLL)

    @pl.when(i == 0)
    def _():
        issue(dcur_ref, 0)

    @pl.when(i + 1 < pl.num_programs(0))
    def _():
        issue(dnext_ref, 1 - cur)

    base = pl.multiple_of(cur * slot_rows, slot_rows)
    pltpu.make_async_copy(ys_hbm.at[pl.ds(0, slot_rows), :], buf.at[pl.ds(base, slot_rows), :],
                          sem.at[cur]).wait()

    mw = mw_ref[...]
    wk = [jnp.broadcast_to(mw[:, kk:kk + 1], (tb, LANES)) for kk in range(TOP_K)]
    cols = []
    for c in range(ROW_SLABS):
        acc = x2_ref[:, c * LANES:(c + 1) * LANES]
        for kk in range(TOP_K):
            rows = buf[pl.ds(base + kk * tb * ROW_SLABS + c, tb, stride=ROW_SLABS), :]
            acc = acc + wk[kk] * rows
        cols.append(acc)
    o_ref[...] = _rms(jnp.concatenate(cols, axis=1), g_ref[...])


def _combine(dest3, x2, mw, g_final, ys):
    t = x2.shape[0]
    nb, _, n = dest3.shape
    tb = n // TOP_K
    row = lambda i: (i, 0)
    dspec = lambda imap: pl.BlockSpec((1, 1, n), imap, memory_space=pltpu.SMEM)
    return pl.pallas_call(
        _combine_body,
        grid=(nb,),
        in_specs=[dspec(lambda i: (i, 0, 0)), dspec(lambda i: (jnp.minimum(i + 1, nb - 1), 0, 0)),
                  pl.BlockSpec((tb, D_MODEL), row), pl.BlockSpec((tb, LANES), row),
                  _const_spec((1, D_MODEL)), pl.BlockSpec(memory_space=pl.ANY)],
        out_specs=pl.BlockSpec((tb, D_MODEL), row),
        out_shape=jax.ShapeDtypeStruct((t, D_MODEL), F32),
        scratch_shapes=[pltpu.VMEM((2 * TOP_K * tb * ROW_SLABS, LANES), F32),
                        pltpu.SemaphoreType.DMA((2,))],
        compiler_params=_params(("arbitrary",)),
        name="combine",
    )(dest3, dest3, x2, mw, g_final.reshape(1, D_MODEL), ys)


def _block_diag_gates(w_rg, w_ig):
    per = GATE_CHUNK // RNN_BLOCK_DIM
    nchunk = w_rg.shape[0] // per
    eye = jnp.eye(per, dtype=w_rg.dtype)

    def bd(w):
        w = w.reshape(nchunk, per, RNN_BLOCK_DIM, RNN_BLOCK_DIM)
        return jnp.einsum('cpij,pq->cpiqj', w, eye).reshape(nchunk, GATE_CHUNK, GATE_CHUNK)

    return jnp.concatenate([bd(w_rg), bd(w_ig)], axis=-1).astype(BF16)


def kernel(x, g_mix, w_in, conv_w, conv_b, w_rg, b_rg, w_ig, b_ig, lru_lambda, b_f, w_lru_out,
           w_fox_out, w_o, g_ffn, w_router, b_router, w_gate_up, b_gate_up, w_down, b_down, g_final):
    bsz, s, d = x.shape
    assert d == D_MODEL
    t = bsz * s
    x2d = x.reshape(t, d)

    o_f = 5 * D_MODEL
    o_g = o_f + N_HEADS
    w_wide = jnp.concatenate([w_in[:, :o_f], w_in[:, o_g:]], axis=1).astype(BF16)
    w_f = jnp.pad(w_in[:, o_f:o_g], ((0, 0), (0, LANES - N_HEADS))).astype(BF16)
    b_f_pad = jnp.pad(b_f, (0, LANES - N_HEADS)).reshape(1, LANES)
    w_r = jnp.pad(w_router, ((0, 0), (0, LANES - N_EXPERTS)))
    w_r_hi = w_r.astype(BF16)
    w_r = jnp.stack([w_r_hi, (w_r - w_r_hi.astype(F32)).astype(BF16)])
    b_r = jnp.pad(b_router, (0, LANES - N_EXPERTS)).reshape(1, LANES)

    lx, lg, q, k, v, gl, gf, cum = _inproj(x2d, g_mix, w_wide, w_f, b_f_pad, s)

    ml = _lru(lx, lg, gl, conv_w, conv_b, _block_diag_gates(w_rg, w_ig), b_rg, b_ig, lru_lambda,
              w_lru_out.astype(BF16), bsz, s)

    ct = cum[:, :N_HEADS].reshape(bsz, s, N_HEADS // HEADS_PER_STEP, HEADS_PER_STEP).transpose(0, 2, 3, 1)
    yf = _attention(q, k, v, ct, bsz, s)

    x2, h2s, route, mw, cnt = _merge(x2d, ml, yf, gf, w_fox_out.astype(BF16), w_o.astype(BF16),
                                     g_ffn, w_r, b_r)

    assert TOK_TILE == ROW_TILE
    counts = cnt[0, :N_EXPERTS]
    tiles_e = (counts + EXP_TILE - 1) // EXP_TILE
    tile_end = jnp.cumsum(tiles_e)
    row_off = (tile_end - tiles_e) * EXP_TILE
    idx = route[:, :TOP_K, :]
    dest = route[:, TOP_K:, :]
    for e in range(N_EXPERTS):
        dest = dest + jnp.where(idx == e, row_off[e], 0)
    dest3 = dest.astype(I32).reshape(route.shape[0], 1, TOP_K * route.shape[2])
    n_tiles = (t * TOP_K) // EXP_TILE + N_EXPERTS
    n_used = tile_end[-1:].astype(I32)
    tile_ids = jnp.minimum(jnp.arange(n_tiles, dtype=I32), n_used[0] - 1)
    tile_expert = jnp.sum(tile_ids[:, None] >= tile_end[None, :], axis=1).astype(I32)

    xs = _scatter(dest3, (row_off + counts).astype(I32), (tiles_e * EXP_TILE - counts).astype(I32),
                  n_used, h2s, n_tiles)
    ys = _experts(tile_expert, n_used, tile_end, xs, w_gate_up, b_gate_up, w_down, b_down, n_tiles)
    out = _combine(dest3, x2, mw, g_final, ys)
    return out.reshape(bsz, s, d)
```

```python
import functools
import math

import jax
import jax.numpy as jnp
from jax import lax
from jax.experimental import pallas as pl
from jax.experimental.pallas import tpu as pltpu

F32 = jnp.float32
BF16 = jnp.bfloat16
I32 = jnp.int32

D_MODEL = 1024
RNN_BLOCK_DIM = 64
LRU_C = 8.0
HEAD_DIM = 64
N_HEADS = 16
N_EXPERTS = 32
TOP_K = 4
D_FF = 1024
SWIGLU_LIMIT = 7.0
SWIGLU_ALPHA = 1.702
RMS_EPS = 1e-6
LOG2E = 1.4426950408889634
Q_SCALE = LOG2E / math.sqrt(HEAD_DIM)
Q_GROUP = 2

LANES = 128
SUBLANES = 8
ROW_SLABS = D_MODEL // LANES
VMEM_LIMIT_BYTES = 56 * 1024 * 1024

PROJ_TILE = 256
TOK_TILE = 512
LRU_TILE = 512
ATT_TILE = 512
HEADS_PER_STEP = 4
KV_CHUNK_TILES = 4
EXP_TILE = 256
ROW_TILE = 512
ISSUE_UNROLL = 4
CONV_WIDTH = 4
GATE_CHUNK = 256


def _params(semantics):
    return pltpu.CompilerParams(dimension_semantics=semantics, vmem_limit_bytes=VMEM_LIMIT_BYTES)


def _rms(x, g):
    return x * lax.rsqrt(jnp.mean(x * x, axis=-1, keepdims=True) + RMS_EPS) * g


def _const_spec(shape):
    nd = len(shape)
    return pl.BlockSpec(shape, lambda *_: (0,) * nd)


def _inproj_body(x_ref, g_ref, w_ref, wf_ref, bf_ref, *rest, steps_per_seq):
    *wide, o_cum, carry = rest
    tm = x_ref.shape[0]

    @pl.when(pl.program_id(0) % steps_per_seq == 0)
    def _():
        carry[...] = jnp.zeros_like(carry)

    h = _rms(x_ref[...], g_ref[...]).astype(BF16)
    for j, o in enumerate(wide):
        acc = jnp.dot(h, w_ref[:, j * D_MODEL:(j + 1) * D_MODEL], preferred_element_type=F32)
        if j == Q_GROUP:
            acc = acc * Q_SCALE
        o[...] = acc.astype(o.dtype)

    z = jnp.dot(h, wf_ref[...], preferred_element_type=F32) + bf_ref[...]
    logf = jnp.minimum(z, 0.0) - jnp.log1p(jnp.exp(-jnp.abs(z)))
    r = lax.broadcasted_iota(I32, (tm, tm), 0)
    c = lax.broadcasted_iota(I32, (tm, tm), 1)
    tri = (r >= c).astype(BF16)
    cum = carry[0:1, :]
    for term in _split3(logf):
        cum = cum + jnp.dot(tri, term.astype(BF16), preferred_element_type=F32)
    o_cum[...] = cum
    carry[...] = jnp.broadcast_to(cum[tm - 1:tm, :], carry.shape)


def _inproj(x2d, g_mix, w_wide, w_f, b_f_pad, s):
    t = x2d.shape[0]
    n_wide = w_wide.shape[1] // D_MODEL
    tm = min(PROJ_TILE, s)
    row = lambda i: (i, 0)
    return pl.pallas_call(
        functools.partial(_inproj_body, steps_per_seq=s // tm),
        grid=(t // tm,),
        in_specs=[pl.BlockSpec((tm, D_MODEL), row), _const_spec((1, D_MODEL)),
                  _const_spec(w_wide.shape), _const_spec(w_f.shape), _const_spec((1, LANES))],
        out_specs=[pl.BlockSpec((tm, D_MODEL), row)] * n_wide + [pl.BlockSpec((tm, LANES), row)],
        out_shape=[jax.ShapeDtypeStruct((t, D_MODEL), BF16)] * n_wide
                  + [jax.ShapeDtypeStruct((t, LANES), F32)],
        scratch_shapes=[pltpu.VMEM((SUBLANES, LANES), F32)],
        compiler_params=_params(("arbitrary",)),
        name="inproj",
    )(x2d, g_mix.reshape(1, D_MODEL), w_wide, w_f, b_f_pad)


def _lru_body(lx_ref, lg_ref, gl_ref, cw_ref, cb_ref, wbd_ref, brg_ref, big_ref, lam_ref, wo_ref,
              o_ref, xbuf, hcar, a_s, b_s, h_s):
    ts = lx_ref.shape[0]

    @pl.when(pl.program_id(1) == 0)
    def _():
        xbuf[0:SUBLANES, :] = jnp.zeros((SUBLANES, D_MODEL), F32)
        hcar[...] = jnp.zeros_like(hcar)

    x = lx_ref[...].astype(F32)
    xbuf[SUBLANES:SUBLANES + ts, :] = x
    xc = cb_ref[...] + cw_ref[CONV_WIDTH - 1:CONV_WIDTH, :] * x
    for k in range(CONV_WIDTH - 1):
        off = SUBLANES - (CONV_WIDTH - 1) + k
        xc = xc + cw_ref[k:k + 1, :] * xbuf[off:off + ts, :]
    xbuf[0:SUBLANES, :] = xbuf[ts:ts + SUBLANES, :]

    xcb = xc.astype(BF16)
    for c in range(D_MODEL // GATE_CHUNK):
        sl = slice(c * GATE_CHUNK, (c + 1) * GATE_CHUNK)
        g = jnp.dot(xcb[:, sl], wbd_ref[c], preferred_element_type=F32)
        rt = jax.nn.sigmoid(g[:, :GATE_CHUNK] + brg_ref[:, sl])
        it = jax.nn.sigmoid(g[:, GATE_CHUNK:] + big_ref[:, sl])
        nl = -lam_ref[:, sl]
        softplus = jnp.maximum(nl, 0.0) + jnp.log1p(jnp.exp(-jnp.abs(nl)))
        a = jnp.exp(-LRU_C * rt * softplus)
        a_s[:, sl] = a
        b_s[:, sl] = jnp.sqrt(1.0 - a * a) * (it * xc[:, sl])

    row = lax.broadcasted_iota(I32, (SUBLANES, D_MODEL), 0)

    def group(gidx, h0):
        off = pl.multiple_of(gidx * SUBLANES, SUBLANES)
        av = a_s[pl.ds(off, SUBLANES), :]
        bv = b_s[pl.ds(off, SUBLANES), :]
        for d in (1, 2, 4):
            keep = row >= d
            a_sh = jnp.where(keep, pltpu.roll(av, d, 0), 1.0)
            b_sh = jnp.where(keep, pltpu.roll(bv, d, 0), 0.0)
            bv = av * b_sh + bv
            av = av * a_sh
        hv = av * h0 + bv
        h_s[pl.ds(off, SUBLANES), :] = hv
        return jnp.broadcast_to(hv[SUBLANES - 1:SUBLANES, :], (SUBLANES, D_MODEL))

    hcar[...] = lax.fori_loop(0, ts // SUBLANES, group, hcar[...])

    y = (h_s[...] * jax.nn.gelu(lg_ref[...].astype(F32), approximate=True)).astype(BF16)
    proj = jnp.dot(y, wo_ref[...], preferred_element_type=F32)
    o_ref[...] = (jax.nn.sigmoid(gl_ref[...].astype(F32)) * proj).astype(o_ref.dtype)


def _lru(lx, lg, gl, conv_w, conv_b, wbd, b_rg, b_ig, lam, w_out, bsz, s):
    ts = min(LRU_TILE, s)
    ns = s // ts
    tile = pl.BlockSpec((ts, D_MODEL), lambda b, i: (b * ns + i, 0))
    vec = lambda a: a.reshape(1, D_MODEL)
    return pl.pallas_call(
        _lru_body,
        grid=(bsz, ns),
        in_specs=[tile, tile, tile, _const_spec((CONV_WIDTH, D_MODEL)), _const_spec((1, D_MODEL)),
                  _const_spec(wbd.shape), _const_spec((1, D_MODEL)), _const_spec((1, D_MODEL)),
                  _const_spec((1, D_MODEL)), _const_spec(w_out.shape)],
        out_specs=tile,
        out_shape=jax.ShapeDtypeStruct((bsz * s, D_MODEL), BF16),
        scratch_shapes=[pltpu.VMEM((ts + SUBLANES, D_MODEL), F32), pltpu.VMEM((SUBLANES, D_MODEL), F32),
                        pltpu.VMEM((ts, D_MODEL), F32), pltpu.VMEM((ts, D_MODEL), F32),
                        pltpu.VMEM((ts, D_MODEL), F32)],
        compiler_params=_params(("arbitrary", "arbitrary")),
        name="lru",
    )(lx, lg, gl, conv_w, vec(conv_b), wbd, vec(b_rg), vec(b_ig), vec(lam), w_out)


def _split3(c):
    hi = c.astype(BF16).astype(F32)
    rest = c - hi
    mid = rest.astype(BF16).astype(F32)
    return hi, mid, rest - mid


def _bias_columns(row_ref, h, start, n, query_side):
    hi, mid, lo = _split3(row_ref[h:h + 1, pl.ds(start, n)] * LOG2E)
    one = jnp.ones_like(hi)
    terms = (hi, mid, lo, one, one, one) if query_side else (one, one, one, -hi, -mid, -lo)
    sub = lax.broadcasted_iota(I32, (SUBLANES, n), 0)
    rows = jnp.zeros((SUBLANES, n), F32)
    for r, term in enumerate(terms):
        rows = jnp.where(sub == r, term, rows)
    above = jnp.zeros((HEAD_DIM, LANES), F32)
    below = jnp.zeros((LANES - HEAD_DIM - SUBLANES, LANES), F32)
    blocks = [jnp.concatenate([above, rows[:, j * LANES:(j + 1) * LANES], below], axis=0).T
              for j in range(n // LANES)]
    return jnp.concatenate(blocks, axis=0)


def _attn_body(q_ref, k_ref, v_ref, cq_ref, ck_ref, o_ref, ka_s, va_s, qa_s, m_s, acc_s):
    tq = q_ref.shape[0]
    tk = tq
    s_len = k_ref.shape[0]
    qi = pl.program_id(2)
    lane = lax.broadcasted_iota(I32, (1, LANES), 1)
    is_head = lane < HEAD_DIM
    ones_col = jnp.where(lane == HEAD_DIM, 1.0, 0.0)

    def head_lanes(x, h):
        g = x[:, (h // 2) * LANES:(h // 2 + 1) * LANES]
        return g if h % 2 == 0 else pltpu.roll(g, HEAD_DIM, 1)

    @pl.when(qi == 0)
    def _():
        def chunk(ci, carry):
            off = pl.multiple_of(ci * tk, tk)
            k2 = k_ref[pl.ds(off, tk), :].astype(F32)
            v2 = v_ref[pl.ds(off, tk), :].astype(F32)
            for h in range(HEADS_PER_STEP):
                ext = _bias_columns(ck_ref, h, off, tk, query_side=False)
                ka_s[h, pl.ds(off, tk), :] = jnp.where(is_head, head_lanes(k2, h), ext).astype(BF16)
                va_s[h, pl.ds(off, tk), :] = jnp.where(is_head, head_lanes(v2, h), ones_col).astype(BF16)
            return carry

        lax.fori_loop(0, s_len // tk, chunk, 0)

    q2 = q_ref[...].astype(F32)
    for h in range(HEADS_PER_STEP):
        ext = _bias_columns(cq_ref, h, 0, tq, query_side=True)
        qa_s[h] = jnp.where(is_head, head_lanes(q2, h), ext).astype(BF16)
    m_s[...] = jnp.full_like(m_s, -jnp.inf)
    acc_s[...] = jnp.zeros_like(acc_s)

    def step(first_key, nk, diag_shift):
        off = pl.multiple_of(first_key, tk)
        for h in range(HEADS_PER_STEP):
            sc = lax.dot_general(qa_s[h], ka_s[h, pl.ds(off, nk), :], (((1,), (1,)), ((), ())),
                                 preferred_element_type=F32)
            if diag_shift is not None:
                r = lax.broadcasted_iota(I32, (tq, tk), 0)
                c = lax.broadcasted_iota(I32, (tq, tk), 1)
                last = jnp.where(c <= r, sc[:, diag_shift:], -jnp.inf)
                sc = jnp.concatenate([sc[:, :diag_shift], last], axis=1) if diag_shift else last
            m_prev = m_s[h]
            m_new = jnp.maximum(m_prev, jnp.max(sc, axis=1, keepdims=True))
            p = jnp.concatenate([jnp.exp2(sc[:, j * LANES:(j + 1) * LANES] - m_new)
                                 for j in range(nk // LANES)], axis=1)
            m_s[h] = m_new
            acc_s[h] = jnp.exp2(m_prev - m_new) * acc_s[h] + jnp.dot(
                p.astype(BF16), va_s[h, pl.ds(off, nk), :], preferred_element_type=F32)

    wide = KV_CHUNK_TILES * tk

    def wide_step(j, carry):
        step(j * wide, wide, None)
        return carry

    lax.fori_loop(0, qi // KV_CHUNK_TILES, wide_step, 0)

    for left in range(KV_CHUNK_TILES):
        @pl.when(qi % KV_CHUNK_TILES == left)
        def _(left=left):
            step((qi - left) * tk, (left + 1) * tk, left * tk)

    groups = []
    for g in range(HEADS_PER_STEP // 2):
        o0 = acc_s[2 * g] / acc_s[2 * g][:, HEAD_DIM:HEAD_DIM + 1]
        o1 = acc_s[2 * g + 1] / acc_s[2 * g + 1][:, HEAD_DIM:HEAD_DIM + 1]
        groups.append(jnp.where(is_head, o0, pltpu.roll(o1, HEAD_DIM, 1)))
    o_ref[...] = jnp.concatenate(groups, axis=1).astype(o_ref.dtype)


def _attention(q, k, v, ct, bsz, s):
    tq = min(ATT_TILE, s)
    nq = s // tq
    hps = HEADS_PER_STEP
    width = hps * HEAD_DIM
    qspec = pl.BlockSpec((tq, width), lambda b, p, i: (b * nq + i, p))
    kvspec = pl.BlockSpec((s, width), lambda b, p, i: (b, p))
    return pl.pallas_call(
        _attn_body,
        grid=(bsz, N_HEADS // hps, nq),
        in_specs=[qspec, kvspec, kvspec,
                  pl.BlockSpec((None, None, hps, tq), lambda b, p, i: (b, p, 0, i)),
                  pl.BlockSpec((None, None, hps, s), lambda b, p, i: (b, p, 0, 0))],
        out_specs=qspec,
        out_shape=jax.ShapeDtypeStruct((bsz * s, D_MODEL), BF16),
        scratch_shapes=[pltpu.VMEM((hps, s, LANES), BF16), pltpu.VMEM((hps, s, LANES), BF16),
                        pltpu.VMEM((hps, tq, LANES), BF16), pltpu.VMEM((hps, tq, LANES), F32),
                        pltpu.VMEM((hps, tq, LANES), F32)],
        compiler_params=_params(("arbitrary", "arbitrary", "arbitrary")),
        name="attn",
    )(q, k, v, ct, ct)


def _merge_body(x_ref, ml_ref, yf_ref, gf_ref, wfo_ref, wo_ref, g_ref, wr_ref, br_ref,
                x2_ref, h2_ref, mi_ref, mw_ref, cnt_ref, running):
    tm = x_ref.shape[0]

    @pl.when(pl.program_id(0) == 0)
    def _():
        running[...] = jnp.zeros_like(running)

    fox = jnp.dot(yf_ref[...], wfo_ref[...], preferred_element_type=F32)
    merged = ml_ref[...].astype(F32) + jax.nn.sigmoid(gf_ref[...].astype(F32)) * fox
    x2 = x_ref[...] + jnp.dot(merged.astype(BF16), wo_ref[...], preferred_element_type=F32)
    x2_ref[...] = x2
    h2 = _rms(x2, g_ref[...])
    for c in range(ROW_SLABS):
        h2_ref[pl.ds(c, tm, stride=ROW_SLABS), :] = h2[:, c * LANES:(c + 1) * LANES]

    h_hi = h2.astype(BF16)
    h_lo = (h2 - h_hi.astype(F32)).astype(BF16)
    logits = (jnp.dot(h_hi, wr_ref[0], preferred_element_type=F32)
              + jnp.dot(h_lo, wr_ref[0], preferred_element_type=F32)
              + jnp.dot(h_hi, wr_ref[1], preferred_element_type=F32)) + br_ref[...]
    lane = lax.broadcasted_iota(I32, (tm, LANES), 1)
    lane_f = lane.astype(F32)
    logits = jnp.where(lane < N_EXPERTS, logits, -jnp.inf)
    vals, idxs = [], []
    for _ in range(TOP_K):
        mx = jnp.max(logits, axis=1, keepdims=True)
        ix = jnp.min(jnp.where(logits == mx, lane_f, float(LANES)), axis=1, keepdims=True).astype(I32)
        vals.append(mx)
        idxs.append(ix)
        logits = jnp.where(lane == ix, -jnp.inf, logits)
    exps = [jnp.exp(vk - vals[0]) for vk in vals]
    denom = exps[0] + exps[1] + exps[2] + exps[3]

    onehot = jnp.zeros((tm, LANES), F32)
    for ix in idxs:
        onehot = onehot + (lane == ix).astype(F32)
    r = lax.broadcasted_iota(I32, (tm, tm), 0)
    c = lax.broadcasted_iota(I32, (tm, tm), 1)
    before = jnp.dot((c < r).astype(BF16), onehot.astype(BF16), preferred_element_type=F32)
    pos = before + running[0:1, :]
    mi = jnp.zeros((tm, LANES), F32)
    mw = jnp.zeros((tm, LANES), F32)
    for kk in range(TOP_K):
        rank = jnp.sum(jnp.where(lane == idxs[kk], pos, 0.0), axis=1, keepdims=True)
        mi = jnp.where(lane == kk, idxs[kk].astype(F32), mi)
        mi = jnp.where(lane == TOP_K + kk, rank, mi)
        mw = jnp.where(lane == kk, exps[kk] / denom, mw)
    mi_ref[...] = mi.T[0:2 * TOP_K, :].astype(I32)
    mw_ref[...] = mw
    total = running[0:1, :] + jnp.sum(onehot, axis=0, keepdims=True)
    running[...] = jnp.broadcast_to(total, running.shape)
    cnt_ref[...] = jnp.broadcast_to(total, cnt_ref.shape).astype(I32)


def _merge(x2d, ml, yf, gf, w_fo, w_o, g_ffn, w_r, b_r):
    t = x2d.shape[0]
    tm = min(TOK_TILE, t)
    row = lambda i: (i, 0)
    tile = pl.BlockSpec((tm, D_MODEL), row)
    meta = pl.BlockSpec((tm, LANES), row)
    return pl.pallas_call(
        _merge_body,
        grid=(t // tm,),
        in_specs=[tile, tile, tile, tile, _const_spec(w_fo.shape), _const_spec(w_o.shape),
                  _const_spec((1, D_MODEL)), _const_spec(w_r.shape), _const_spec((1, LANES))],
        out_specs=[tile, pl.BlockSpec((tm * ROW_SLABS, LANES), row),
                   pl.BlockSpec((None, 2 * TOP_K, tm), lambda i: (i, 0, 0)), meta,
                   _const_spec((SUBLANES, LANES))],
        out_shape=[jax.ShapeDtypeStruct((t, D_MODEL), F32),
                   jax.ShapeDtypeStruct((t * ROW_SLABS, LANES), F32),
                   jax.ShapeDtypeStruct((t // tm, 2 * TOP_K, tm), I32),
                   jax.ShapeDtypeStruct((t, LANES), F32),
                   jax.ShapeDtypeStruct((SUBLANES, LANES), I32)],
        scratch_shapes=[pltpu.VMEM((SUBLANES, LANES), F32)],
        compiler_params=_params(("arbitrary",)),
        name="merge",
    )(x2d, ml, yf, gf, w_fo, w_o, g_ffn.reshape(1, D_MODEL), w_r, b_r)


def _row_slab(ref, row, base=0):
    return ref.at[pl.ds(pl.multiple_of(base + row * ROW_SLABS, ROW_SLABS), ROW_SLABS), :]


def _scatter_body(pad_start_ref, pad_ref, nu_ref, dest_ref, h2_ref, xs_hbm, zbuf, sem, zsem, *, n_tiles):
    tb = dest_ref.shape[-1] // TOP_K
    tile_rows = EXP_TILE * ROW_SLABS

    @pl.when(pl.program_id(0) == 0)
    def _():
        zbuf[...] = jnp.zeros_like(zbuf)

        def pad_pieces():
            for e in range(N_EXPERTS):
                pad = pad_ref[e]
                for bit in reversed(range(EXP_TILE.bit_length() - 1)):
                    size = 1 << bit
                    before = (pad >> (bit + 1)) << (bit + 1)
                    first = pl.multiple_of((pad_start_ref[e] + before) * ROW_SLABS, ROW_SLABS)
                    cp = pltpu.make_async_copy(zbuf.at[pl.ds(0, size * ROW_SLABS), :],
                                               xs_hbm.at[pl.ds(first, size * ROW_SLABS), :], zsem)
                    yield (pad & size) != 0, cp

        def tail_copy(j):
            return pltpu.make_async_copy(
                zbuf, xs_hbm.at[pl.ds(pl.multiple_of(j * tile_rows, tile_rows), tile_rows), :], zsem)

        for cond, cp in pad_pieces():
            pl.when(cond)(cp.start)
        lax.fori_loop(nu_ref[0], n_tiles, lambda j, c: (tail_copy(j).start(), c)[1], 0)
        for cond, cp in pad_pieces():
            pl.when(cond)(cp.wait)
        lax.fori_loop(nu_ref[0], n_tiles, lambda j, c: (tail_copy(j).wait(), c)[1], 0)

    def issue(j, carry):
        for kk in range(TOP_K):
            d = dest_ref[0, 0, kk * tb + j]
            pltpu.make_async_copy(_row_slab(h2_ref, j), _row_slab(xs_hbm, d), sem).start(
                priority=kk % 2)
        return carry

    lax.fori_loop(0, tb, issue, 0, unroll=ISSUE_UNROLL)
    n = tb * TOP_K * ROW_SLABS
    pltpu.make_async_copy(xs_hbm.at[pl.ds(0, n), :], xs_hbm.at[pl.ds(0, n), :], sem).wait()


def _scatter(dest3, pad_start, pad, n_used, h2s, n_tiles):
    nb, _, n = dest3.shape
    tb = n // TOP_K
    grid_spec = pltpu.PrefetchScalarGridSpec(
        num_scalar_prefetch=3,
        grid=(nb,),
        in_specs=[pl.BlockSpec((1, 1, n), lambda i, *_: (i, 0, 0), memory_space=pltpu.SMEM),
                  pl.BlockSpec((tb * ROW_SLABS, LANES), lambda i, *_: (i, 0))],
        out_specs=pl.BlockSpec(memory_space=pl.ANY),
        scratch_shapes=[pltpu.VMEM((EXP_TILE * ROW_SLABS, LANES), F32), pltpu.SemaphoreType.DMA(()),
                        pltpu.SemaphoreType.DMA(())],
    )
    return pl.pallas_call(
        functools.partial(_scatter_body, n_tiles=n_tiles),
        grid_spec=grid_spec,
        out_shape=jax.ShapeDtypeStruct((n_tiles * EXP_TILE * ROW_SLABS, LANES), F32),
        compiler_params=_params(("arbitrary",)),
        name="scatter",
    )(pad_start, pad, n_used, dest3, h2s)


def _expert_body(te_ref, first_ref, slot_ref, next_ref, nu_ref, xs_ref, wgu_hbm, bgu_ref, wd_hbm, bd_ref,
                 ys_ref, wgu_f, wd_f, wgu_bf, wd_bf, sems):
    i = pl.program_id(0)
    tm = xs_ref.shape[0] // ROW_SLABS

    def weight_copies(e, s):
        return (pltpu.make_async_copy(wgu_hbm.at[e], wgu_f.at[s], sems.at[0, s]),
                pltpu.make_async_copy(wd_hbm.at[e], wd_f.at[s], sems.at[1, s]))

    @pl.when(i == 0)
    def _():
        for cp in weight_copies(te_ref[0], 0):
            cp.start()

    @pl.when(first_ref[i] == 1)
    def _():
        s = slot_ref[i]
        for cp in weight_copies(te_ref[i], s):
            cp.wait()

        @pl.when(next_ref[i] >= 0)
        def _():
            for cp in weight_copies(next_ref[i], 1 - s):
                cp.start()

        wgu_bf[...] = wgu_f[s].astype(BF16)
        wd_bf[...] = wd_f[s].astype(BF16)

    @pl.when(i < nu_ref[0])
    def _():
        x = jnp.concatenate([xs_ref[pl.ds(c, tm, stride=ROW_SLABS), :] for c in range(ROW_SLABS)],
                            axis=1).astype(BF16)
        gu = jnp.dot(x, wgu_bf[...], preferred_element_type=F32) + bgu_ref[...]
        gate = jnp.minimum(gu[:, :D_FF], SWIGLU_LIMIT)
        up = jnp.clip(gu[:, D_FF:], -SWIGLU_LIMIT, SWIGLU_LIMIT)
        glu = gate * jax.nn.sigmoid(SWIGLU_ALPHA * gate)
        y = jnp.dot(((up + 1.0) * glu).astype(BF16), wd_bf[...],
                    preferred_element_type=F32) + bd_ref[...]
        for c in range(ROW_SLABS):
            ys_ref[pl.ds(c, tm, stride=ROW_SLABS), :] = y[:, c * LANES:(c + 1) * LANES]


def _experts(tile_expert, n_used, tile_end, xs, w_gate_up, b_gate_up, w_down, b_down, n_tiles):
    tm = EXP_TILE
    tile = jnp.arange(n_tiles, dtype=I32)
    used = tile < n_used[0]
    prev = jnp.concatenate([jnp.full((1,), -1, I32), tile_expert[:-1]])
    first = jnp.logical_and(used, tile_expert != prev).astype(I32)
    slot = ((jnp.cumsum(first) - 1) % 2).astype(I32)
    next_tile = tile_end[tile_expert].astype(I32)
    nxt = jnp.where(next_tile < n_used[0], tile_expert[jnp.minimum(next_tile, n_tiles - 1)], -1).astype(I32)

    rows = lambda i, te, fi, sl, nx, nu: (jnp.minimum(i, nu[0] - 1), 0)
    bsel = lambda i, te, fi, sl, nx, nu: (te[i], 0, 0)
    grid_spec = pltpu.PrefetchScalarGridSpec(
        num_scalar_prefetch=5,
        grid=(n_tiles,),
        in_specs=[pl.BlockSpec((tm * ROW_SLABS, LANES), rows),
                  pl.BlockSpec(memory_space=pl.ANY),
                  pl.BlockSpec((None, 1, 2 * D_FF), bsel),
                  pl.BlockSpec(memory_space=pl.ANY),
                  pl.BlockSpec((None, 1, D_MODEL), bsel)],
        out_specs=pl.BlockSpec((tm * ROW_SLABS, LANES), rows),
        scratch_shapes=[pltpu.VMEM((2, D_MODEL, 2 * D_FF), F32), pltpu.VMEM((2, D_FF, D_MODEL), F32),
                        pltpu.VMEM((D_MODEL, 2 * D_FF), BF16), pltpu.VMEM((D_FF, D_MODEL), BF16),
                        pltpu.SemaphoreType.DMA((2, 2))],
    )
    return pl.pallas_call(
        _expert_body,
        grid_spec=grid_spec,
        out_shape=jax.ShapeDtypeStruct(xs.shape, F32),
        input_output_aliases={5: 0},
        compiler_params=_params(("arbitrary",)),
        name="experts",
    )(tile_expert, first, slot, nxt, n_used, xs, w_gate_up, b_gate_up.reshape(N_EXPERTS, 1, 2 * D_FF),
      w_down, b_down.reshape(N_EXPERTS, 1, D_MODEL))


def _combine_body(dcur_ref, dnext_ref, x2_ref, mw_ref, g_ref, ys_hbm, o_ref, buf, sem):
    i = pl.program_id(0)
    tb = x2_ref.shape[0]
    slot_rows = TOP_K * tb * ROW_SLABS
    cur = i % 2

    def issue(d_ref, slot):
        base = pl.multiple_of(slot * slot_rows, slot_rows)

        def body(j, carry):
            for kk in range(TOP_K):
                d = d_ref[0, 0, kk * tb + j]
                pltpu.make_async_copy(_row_slab(ys_hbm, d), _row_slab(buf, kk * tb + j, base),
                                      sem.at[slot]).start(priority=kk % 2)
            return carry

        lax.fori_loop(0, tb, body, 0, unroll=ISSUE_UNROLL)

    @pl.when(i == 0)
    def _():
        issue(dcur_ref, 0)

    @pl.when(i + 1 < pl.num_programs(0))
    def _():
        issue(dnext_ref, 1 - cur)

    base = pl.multiple_of(cur * slot_rows, slot_rows)
    pltpu.make_async_copy(ys_hbm.at[pl.ds(0, slot_rows), :], buf.at[pl.ds(base, slot_rows), :],
                          sem.at[cur]).wait()

    mw = mw_ref[...]
    wk = [jnp.broadcast_to(mw[:, kk:kk + 1], (tb, LANES)) for kk in range(TOP_K)]
    cols = []
    for c in range(ROW_SLABS):
        acc = x2_ref[:, c * LANES:(c + 1) * LANES]
        for kk in range(TOP_K):
            rows = buf[pl.ds(base + kk * tb * ROW_SLABS + c, tb, stride=ROW_SLABS), :]
            acc = acc + wk[kk] * rows
        cols.append(acc)
    o_ref[...] = _rms(jnp.concatenate(cols, axis=1), g_ref[...])


def _combine(dest3, x2, mw, g_final, ys):
    t = x2.shape[0]
    nb, _, n = dest3.shape
    tb = n // TOP_K
    row = lambda i: (i, 0)
    dspec = lambda imap: pl.BlockSpec((1, 1, n), imap, memory_space=pltpu.SMEM)
    return pl.pallas_call(
        _combine_body,
        grid=(nb,),
        in_specs=[dspec(lambda i: (i, 0, 0)), dspec(lambda i: (jnp.minimum(i + 1, nb - 1), 0, 0)),
                  pl.BlockSpec((tb, D_MODEL), row), pl.BlockSpec((tb, LANES), row),
                  _const_spec((1, D_MODEL)), pl.BlockSpec(memory_space=pl.ANY)],
        out_specs=pl.BlockSpec((tb, D_MODEL), row),
        out_shape=jax.ShapeDtypeStruct((t, D_MODEL), F32),
        scratch_shapes=[pltpu.VMEM((2 * TOP_K * tb * ROW_SLABS, LANES), F32),
                        pltpu.SemaphoreType.DMA((2,))],
        compiler_params=_params(("arbitrary",)),
        name="combine",
    )(dest3, dest3, x2, mw, g_final.reshape(1, D_MODEL), ys)


def _block_diag_gates(w_rg, w_ig):
    per = GATE_CHUNK // RNN_BLOCK_DIM
    nchunk = w_rg.shape[0] // per
    eye = jnp.eye(per, dtype=w_rg.dtype)

    def bd(w):
        w = w.reshape(nchunk, per, RNN_BLOCK_DIM, RNN_BLOCK_DIM)
        return jnp.einsum('cpij,pq->cpiqj', w, eye).reshape(nchunk, GATE_CHUNK, GATE_CHUNK)

    return jnp.concatenate([bd(w_rg), bd(w_ig)], axis=-1).astype(BF16)


def kernel(x, g_mix, w_in, conv_w, conv_b, w_rg, b_rg, w_ig, b_ig, lru_lambda, b_f, w_lru_out,
           w_fox_out, w_o, g_ffn, w_router, b_router, w_gate_up, b_gate_up, w_down, b_down, g_final):
    bsz, s, d = x.shape
    assert d == D_MODEL
    t = bsz * s
    x2d = x.reshape(t, d)

    o_f = 5 * D_MODEL
    o_g = o_f + N_HEADS
    w_wide = jnp.concatenate([w_in[:, :o_f], w_in[:, o_g:]], axis=1).astype(BF16)
    w_f = jnp.pad(w_in[:, o_f:o_g], ((0, 0), (0, LANES - N_HEADS))).astype(BF16)
    b_f_pad = jnp.pad(b_f, (0, LANES - N_HEADS)).reshape(1, LANES)
    w_r = jnp.pad(w_router, ((0, 0), (0, LANES - N_EXPERTS)))
    w_r_hi = w_r.astype(BF16)
    w_r = jnp.stack([w_r_hi, (w_r - w_r_hi.astype(F32)).astype(BF16)])
    b_r = jnp.pad(b_router, (0, LANES - N_EXPERTS)).reshape(1, LANES)

    lx, lg, q, k, v, gl, gf, cum = _inproj(x2d, g_mix, w_wide, w_f, b_f_pad, s)

    ml = _lru(lx, lg, gl, conv_w, conv_b, _block_diag_gates(w_rg, w_ig), b_rg, b_ig, lru_lambda,
              w_lru_out.astype(BF16), bsz, s)

    ct = cum[:, :N_HEADS].reshape(bsz, s, N_HEADS // HEADS_PER_STEP, HEADS_PER_STEP).transpose(0, 2, 3, 1)
    yf = _attention(q, k, v, ct, bsz, s)

    x2, h2s, route, mw, cnt = _merge(x2d, ml, yf, gf, w_fox_out.astype(BF16), w_o.astype(BF16),
                                     g_ffn, w_r, b_r)

    assert TOK_TILE == ROW_TILE
    counts = cnt[0, :N_EXPERTS]
    tiles_e = (counts + EXP_TILE - 1) // EXP_TILE
    tile_end = jnp.cumsum(tiles_e)
    row_off = (tile_end - tiles_e) * EXP_TILE
    idx = route[:, :TOP_K, :]
    dest = route[:, TOP_K:, :]
    for e in range(N_EXPERTS):
        dest = dest + jnp.where(idx == e, row_off[e], 0)
    dest3 = dest.astype(I32).reshape(route.shape[0], 1, TOP_K * route.shape[2])
    n_tiles = (t * TOP_K) // EXP_TILE + N_EXPERTS
    n_used = tile_end[-1:].astype(I32)
    tile_ids = jnp.minimum(jnp.arange(n_tiles, dtype=I32), n_used[0] - 1)
    tile_expert = jnp.sum(tile_ids[:, None] >= tile_end[None, :], axis=1).astype(I32)

    xs = _scatter(dest3, (row_off + counts).astype(I32), (tiles_e * EXP_TILE - counts).astype(I32),
                  n_used, h2s, n_tiles)
    ys = _experts(tile_expert, n_used, tile_end, xs, w_gate_up, b_gate_up, w_down, b_down, n_tiles)
    out = _combine(dest3, x2, mw, g_final, ys)
    return out.reshape(bsz, s, d)
```
